```python
import jax, jax.numpy as jnp
from jax import lax
import numpy as np

D_MODEL = 2048
BATCH = 1
SEQ = 16384
DEPTH = 1

CHUNK = 64
N_MEM = 256
RMS_EPS = 1e-6

RW_HEADS = 16
RW_HEAD_DIM = 64
RW_WIDTH = RW_HEADS * RW_HEAD_DIM
RW_DECAY_LORA = 64
RW_AAA_LORA = 64
RW_GATE_LORA = 160
RW_GN_EPS = 64e-5
RW_COLS = 3 * RW_WIDTH + RW_DECAY_LORA + RW_AAA_LORA + RW_GATE_LORA

GDN_QK_HEADS = 4
GDN_V_HEADS = 8
GDN_HEAD_DIM = 128
GDN_QK_WIDTH = GDN_QK_HEADS * GDN_HEAD_DIM
GDN_V_WIDTH = GDN_V_HEADS * GDN_HEAD_DIM
GDN_CONV = 4
GDN_CONV_CH = 2 * GDN_QK_WIDTH + GDN_V_WIDTH
GDN_COLS = GDN_CONV_CH + GDN_V_WIDTH + 2 * GDN_V_HEADS

N_BRANCH = 2
BRANCH_WIDTH = 1024
GATE_COLS = N_BRANCH * D_MODEL
IN_COLS = RW_COLS + GDN_COLS + GATE_COLS

MEM_HEADS = 4
MEM_HEAD_DIM = D_MODEL // MEM_HEADS

D_FF = 5632

kernel_name = "rwkv7_gdn_macaron_hybrid"


def _split(t, sizes):
    return jnp.split(t, [int(s) for s in np.cumsum(sizes)[:-1]], axis=-1)


def _rmsnorm(x, g, eps=RMS_EPS):
    xf = x.astype(jnp.float32)
    y = xf * lax.rsqrt(jnp.mean(xf * xf, axis=-1, keepdims=True) + eps)
    return (y * g.astype(jnp.float32)).astype(x.dtype)


def _l2norm(x, eps=1e-6):
    xf = x.astype(jnp.float32)
    return xf * lax.rsqrt(jnp.sum(xf * xf, axis=-1, keepdims=True) + eps)


def _swiglu_ffn(x, norm_g, w_gate, w_up, w_down):
    h = _rmsnorm(x, norm_g)
    return (jax.nn.silu(h @ w_gate) * (h @ w_up)) @ w_down


def _token_shift(x):
    return jnp.pad(x, ((0, 0), (1, 0), (0, 0)))[:, :-1]


def _causal_depthwise_conv(x, w):
    K, C = w.shape
    return lax.conv_general_dilated(
        x, w[:, None, :], window_strides=(1,), padding=[(K - 1, 0)],
        dimension_numbers=("NWC", "WIO", "NWC"), feature_group_count=C)


def _rwkv7_recurrence(r, w, k, v, a, b):
    B_, T, H, N = r.shape

    def step(S, inp):
        r_t, w_t, k_t, v_t, a_t, b_t = inp
        sa = jnp.einsum("bhij,bhj->bhi", S, a_t)
        S = S * w_t[:, :, None, :] + sa[..., None] * b_t[:, :, None, :] + v_t[..., None] * k_t[:, :, None, :]
        return S, jnp.einsum("bhij,bhj->bhi", S, r_t)

    xs = tuple(jnp.moveaxis(t, 1, 0) for t in (r, w, k, v, a, b))
    _, y = lax.scan(step, jnp.zeros((B_, H, N, N), jnp.float32), xs)
    return jnp.moveaxis(y, 0, 1)


def _rwkv7_branch(cols, shift_mu, w0, w2, a0, a2, g2, k_k, k_a, r_k, ln_w, ln_b):
    B_, T, _ = cols.shape
    cols = cols + (_token_shift(cols) - cols) * shift_mu
    r, k, v, wl, al, gl = _split(cols, (RW_WIDTH, RW_WIDTH, RW_WIDTH, RW_DECAY_LORA, RW_AAA_LORA, RW_GATE_LORA))
    w_log = -jax.nn.softplus(-(w0 + jnp.tanh(wl) @ w2)) - 0.5
    a = jax.nn.sigmoid(a0 + al @ a2)
    g = jax.nn.sigmoid(gl) @ g2

    def heads(t):
        return t.reshape(B_, T, RW_HEADS, RW_HEAD_DIM).astype(jnp.float32)

    kk = _l2norm(heads(k * k_k))
    k = k * (1 + (a - 1) * k_a)
    r_h, k_h, v_h, a_h = heads(r), heads(k), heads(v), heads(a)
    decay = jnp.exp(-jnp.exp(heads(w_log)))
    y = _rwkv7_recurrence(r_h, decay, k_h, v_h, -kk, kk * a_h)
    mu = jnp.mean(y, axis=-1, keepdims=True)
    var = jnp.mean(jnp.square(y - mu), axis=-1, keepdims=True)
    y = ((y - mu) * lax.rsqrt(var + RW_GN_EPS)).reshape(B_, T, RW_WIDTH)
    y = y * ln_w.astype(jnp.float32) + ln_b.astype(jnp.float32)
    bonus = jnp.sum(r_h * k_h * r_k.astype(jnp.float32), axis=-1, keepdims=True) * v_h
    y = y + bonus.reshape(B_, T, RW_WIDTH)
    return (y * g.astype(jnp.float32)).astype(cols.dtype)


def _chunked_gated_delta_rule(q, k, v, g, beta):
    B_, T, H, Dk = q.shape
    Dv = v.shape[-1]
    nc = T // CHUNK

    def to_chunks(t):
        return t.reshape((B_, nc, CHUNK, H) + t.shape[3:]).swapaxes(2, 3)

    q, k, v, g, beta = map(to_chunks, (q, k, v, g, beta))
    gc = jnp.cumsum(g, axis=-1)
    idx = jnp.arange(CHUNK)
    lower = idx[:, None] >= idx[None, :]
    strict = idx[:, None] > idx[None, :]
    diff = gc[..., :, None] - gc[..., None, :]
    decay = jnp.where(lower, jnp.exp(jnp.where(lower, diff, 0.0)), 0.0)
    k_beta = k * beta[..., None]
    v_beta = v * beta[..., None]
    m = jnp.where(strict, jnp.einsum("bnhid,bnhjd->bnhij", k_beta, k) * decay, 0.0)
    tri = m + jnp.eye(CHUNK, dtype=m.dtype)
    u = lax.linalg.triangular_solve(tri, v_beta, left_side=True, lower=True, unit_diagonal=True)
    w = lax.linalg.triangular_solve(tri, k_beta * jnp.exp(gc)[..., None], left_side=True, lower=True, unit_diagonal=True)
    a_qk = jnp.where(lower, jnp.einsum("bnhid,bnhjd->bnhij", q, k) * decay, 0.0)

    def step(S, inp):
        q_c, k_c, u_c, w_c, a_c, gc_c = inp
        v_new = u_c - jnp.einsum("bhld,bhde->bhle", w_c, S)
        o = (jnp.einsum("bhld,bhde->bhle", q_c * jnp.exp(gc_c)[..., None], S)
             + jnp.einsum("bhij,bhje->bhie", a_c, v_new))
        g_last = gc_c[..., -1]
        S = (S * jnp.exp(g_last)[..., None, None]
             + jnp.einsum("bhld,bhle->bhde", k_c * jnp.exp(g_last[..., None] - gc_c)[..., None], v_new))
        return S, o

    xs = tuple(jnp.moveaxis(t, 1, 0) for t in (q, k, u, w, a_qk, gc))
    _, o = lax.scan(step, jnp.zeros((B_, H, Dk, Dv), jnp.float32), xs)
    return o.transpose(1, 0, 3, 2, 4).reshape(B_, T, H, Dv)


def _gdn_branch(cols, conv_w, a_log, dt_bias, norm_w):
    B_, T, _ = cols.shape
    qkv, z, b_raw, a_raw = _split(cols, (GDN_CONV_CH, GDN_V_WIDTH, GDN_V_HEADS, GDN_V_HEADS))
    qkv = jax.nn.silu(_causal_depthwise_conv(qkv.astype(jnp.float32), conv_w.astype(jnp.float32)))
    q, k, v = _split(qkv, (GDN_QK_WIDTH, GDN_QK_WIDTH, GDN_V_WIDTH))
    rep = GDN_V_HEADS // GDN_QK_HEADS
    q = jnp.repeat(_l2norm(q.reshape(B_, T, GDN_QK_HEADS, GDN_HEAD_DIM)), rep, axis=2) * (GDN_HEAD_DIM ** -0.5)
    k = jnp.repeat(_l2norm(k.reshape(B_, T, GDN_QK_HEADS, GDN_HEAD_DIM)), rep, axis=2)
    v = v.reshape(B_, T, GDN_V_HEADS, GDN_HEAD_DIM)
    beta = jax.nn.sigmoid(b_raw.astype(jnp.float32))
    g = -jnp.exp(a_log.astype(jnp.float32)) * jax.nn.softplus(a_raw.astype(jnp.float32) + dt_bias.astype(jnp.float32))
    o = _chunked_gated_delta_rule(q, k, v, g, beta)
    z = z.reshape(B_, T, GDN_V_HEADS, GDN_HEAD_DIM).astype(jnp.float32)
    o = o * lax.rsqrt(jnp.mean(o * o, axis=-1, keepdims=True) + RMS_EPS) * norm_w.astype(jnp.float32) * jax.nn.silu(z)
    return o.reshape(B_, T, GDN_V_WIDTH).astype(cols.dtype)


def _memory_cross_attention(h, mem, q_norm, kv_norm, w_mq, w_mk, w_mv, w_mo):
    B_, T, _ = h.shape
    hq = _rmsnorm(h, q_norm)
    mkv = _rmsnorm(mem, kv_norm)
    q = (hq @ w_mq).reshape(B_, T, MEM_HEADS, MEM_HEAD_DIM)
    k = (mkv @ w_mk).reshape(B_, -1, MEM_HEADS, MEM_HEAD_DIM)
    v = (mkv @ w_mv).reshape(B_, -1, MEM_HEADS, MEM_HEAD_DIM)
    s = jnp.einsum("bthd,bmhd->bhtm", q, k).astype(jnp.float32) * (MEM_HEAD_DIM ** -0.5)
    p = jax.nn.softmax(s, axis=-1).astype(v.dtype)
    o = jnp.einsum("bhtm,bmhd->bthd", p, v).reshape(B_, T, D_MODEL)
    return o @ w_mo


def setup_inputs(seed: int = 0) -> dict:
    key = jax.random.key(seed)
    ks = iter(jax.random.split(key, 48))
    L = DEPTH

    def nrm(shape, scale):
        return jax.random.normal(next(ks), shape, jnp.float32) * scale

    def gain(shape):
        return 1.0 + nrm(shape, 0.02)

    def unif(shape, lo, hi):
        return jax.random.uniform(next(ks), shape, jnp.float32, lo, hi)

    dt = jnp.exp(unif((L, GDN_V_HEADS), float(np.log(1e-3)), float(np.log(1e-1))))
    return {
        "x": nrm((BATCH, SEQ, D_MODEL), 1.0),
        "mem": nrm((BATCH, N_MEM, D_MODEL), 1.0),
        "ffn1_norm": gain((L, D_MODEL)),
        "ffn1_w_gate": nrm((L, D_MODEL, D_FF), D_MODEL ** -0.5),
        "ffn1_w_up": nrm((L, D_MODEL, D_FF), D_MODEL ** -0.5),
        "ffn1_w_down": nrm((L, D_FF, D_MODEL), D_FF ** -0.5),
        "mix_norm": gain((L, D_MODEL)),
        "w_in": nrm((L, D_MODEL, IN_COLS), D_MODEL ** -0.5),
        "rw_shift_mu": unif((L, RW_COLS), 0.0, 1.0),
        "rw_w0": unif((L, RW_WIDTH), -6.0, -1.0),
        "rw_w2": nrm((L, RW_DECAY_LORA, RW_WIDTH), 0.1 * RW_DECAY_LORA ** -0.5),
        "rw_a0": nrm((L, RW_WIDTH), 0.1),
        "rw_a2": nrm((L, RW_AAA_LORA, RW_WIDTH), 0.1 * RW_AAA_LORA ** -0.5),
        "rw_g2": nrm((L, RW_GATE_LORA, RW_WIDTH), RW_GATE_LORA ** -0.5),
        "rw_k_k": 0.85 + nrm((L, RW_WIDTH), 0.02),
        "rw_k_a": 1.0 + nrm((L, RW_WIDTH), 0.02),
        "rw_r_k": nrm((L, RW_HEADS, RW_HEAD_DIM), 0.1),
        "rw_ln_w": gain((L, RW_WIDTH)),
        "rw_ln_b": nrm((L, RW_WIDTH), 0.01),
        "gdn_conv_w": nrm((L, GDN_CONV, GDN_CONV_CH), GDN_CONV ** -0.5),
        "gdn_a_log": jnp.log(unif((L, GDN_V_HEADS), 1.0, 16.0)),
        "gdn_dt_bias": dt + jnp.log(-jnp.expm1(-dt)),
        "gdn_norm_w": gain((L, GDN_HEAD_DIM)),
        "w_lift": nrm((L, N_BRANCH, BRANCH_WIDTH, D_MODEL), BRANCH_WIDTH ** -0.5),
        "w_out": nrm((L, D_MODEL, D_MODEL), D_MODEL ** -0.5),
        "mem_q_norm": gain((L, D_MODEL)),
        "mem_kv_norm": gain((L, D_MODEL)),
        "w_mq": nrm((L, D_MODEL, D_MODEL), D_MODEL ** -0.5),
        "w_mk": nrm((L, D_MODEL, D_MODEL), D_MODEL ** -0.5),
        "w_mv": nrm((L, D_MODEL, D_MODEL), D_MODEL ** -0.5),
        "w_mo": nrm((L, D_MODEL, D_MODEL), D_MODEL ** -0.5),
        "ffn2_norm": gain((L, D_MODEL)),
        "ffn2_w_gate": nrm((L, D_MODEL, D_FF), D_MODEL ** -0.5),
        "ffn2_w_up": nrm((L, D_MODEL, D_FF), D_MODEL ** -0.5),
        "ffn2_w_down": nrm((L, D_FF, D_MODEL), D_FF ** -0.5),
        "final_norm": gain((D_MODEL,)),
    }


def reference(x, mem, ffn1_norm, ffn1_w_gate, ffn1_w_up, ffn1_w_down,
              mix_norm, w_in, rw_shift_mu, rw_w0, rw_w2, rw_a0, rw_a2, rw_g2,
              rw_k_k, rw_k_a, rw_r_k, rw_ln_w, rw_ln_b,
              gdn_conv_w, gdn_a_log, gdn_dt_bias, gdn_norm_w,
              w_lift, w_out,
              mem_q_norm, mem_kv_norm, w_mq, w_mk, w_mv, w_mo,
              ffn2_norm, ffn2_w_gate, ffn2_w_up, ffn2_w_down, final_norm):
    B_, T, _ = x.shape
    h = x
    for l in range(DEPTH):
        h = h + 0.5 * _swiglu_ffn(h, ffn1_norm[l], ffn1_w_gate[l], ffn1_w_up[l], ffn1_w_down[l])
        u = _rmsnorm(h, mix_norm[l])
        proj = u @ w_in[l]
        rw_cols, gdn_cols, gate_cols = _split(proj, (RW_COLS, GDN_COLS, GATE_COLS))
        y_rw = _rwkv7_branch(rw_cols, rw_shift_mu[l], rw_w0[l], rw_w2[l], rw_a0[l], rw_a2[l], rw_g2[l],
                             rw_k_k[l], rw_k_a[l], rw_r_k[l], rw_ln_w[l], rw_ln_b[l])
        y_gdn = _gdn_branch(gdn_cols, gdn_conv_w[l], gdn_a_log[l], gdn_dt_bias[l], gdn_norm_w[l])
        gates = jax.nn.sigmoid(gate_cols.reshape(B_, T, N_BRANCH, D_MODEL))
        merged = (gates[:, :, 0] * (y_rw @ w_lift[l, 0])
                  + gates[:, :, 1] * (y_gdn @ w_lift[l, 1]))
        h = h + merged @ w_out[l]
        h = h + _memory_cross_attention(h, mem, mem_q_norm[l], mem_kv_norm[l], w_mq[l], w_mk[l], w_mv[l], w_mo[l])
        h = h + 0.5 * _swiglu_ffn(h, ffn2_norm[l], ffn2_w_gate[l], ffn2_w_up[l], ffn2_w_down[l])
    return _rmsnorm(h, final_norm)
```

```python
import functools

import jax
import jax.numpy as jnp
from jax import lax
from jax.experimental import pallas as pl
from jax.experimental.pallas import tpu as pltpu

F32 = jnp.float32
BF16 = jnp.bfloat16

D_MODEL = 2048
D_FF = 5632
RMS_EPS = 1e-6

RW_HEADS = 16
RW_HEAD_DIM = 64
RW_WIDTH = 1024
RW_DECAY_LORA = 64
RW_AAA_LORA = 64
RW_GATE_LORA = 160
RW_GN_EPS = 64e-5
RW_COLS = 3 * RW_WIDTH + RW_DECAY_LORA + RW_AAA_LORA + RW_GATE_LORA
LANES = 128
RW_OFF_WL = 3 * RW_WIDTH
RW_OFF_AL = RW_OFF_WL + LANES
RW_OFF_GL = RW_OFF_AL + LANES
RW_PAD_COLS = RW_OFF_GL + 2 * LANES

GDN_QK_HEADS = 4
GDN_V_HEADS = 8
GDN_HEAD_DIM = 128
GDN_QK_WIDTH = 512
GDN_V_WIDTH = 1024
GDN_CONV = 4
GDN_CONV_CH = 2 * GDN_QK_WIDTH + GDN_V_WIDTH
GDN_COLS = GDN_CONV_CH + GDN_V_WIDTH + 2 * GDN_V_HEADS
GDN_OFF_Z = GDN_CONV_CH
GDN_OFF_BA = GDN_CONV_CH + GDN_V_WIDTH
GDN_PAD_COLS = GDN_OFF_BA + LANES

MEM_HEADS = 4
MEM_HEAD_DIM = D_MODEL // MEM_HEADS

CHUNK = 64
SUBLANES = 8
VMEM_LIMIT = 56 * 1024 * 1024

_DIMS = {
    "nn": (((1,), (0,)), ((), ())),
    "nt": (((1,), (1,)), ((), ())),
    "tn": (((0,), (0,)), ((), ())),
}


def _split_bf16(x, n):
    if x.dtype == BF16:
        return [x]
    parts = []
    r = x
    for i in range(n):
        p = r.astype(BF16)
        parts.append(p)
        if i + 1 < n:
            r = r - p.astype(F32)
    return parts


def _mm(a, b, dims="nn", pa=1, pb=1):
    ap = _split_bf16(a, pa)
    bp = _split_bf16(b, pb)
    n = max(len(ap), len(bp))
    acc = None
    for i, ai in enumerate(ap):
        for j, bj in enumerate(bp):
            if i + j < n:
                t = lax.dot_general(ai, bj, _DIMS[dims], preferred_element_type=F32)
                acc = t if acc is None else acc + t
    return acc


def _rms(x, g, eps=RMS_EPS):
    ms = jnp.mean(x * x, axis=-1, keepdims=True)
    return x * lax.rsqrt(ms + eps) * g


def _sigmoid(x):
    return 1.0 / (1.0 + jnp.exp(-x))


def _softplus(x):
    return jnp.maximum(x, 0.0) + jnp.log(1.0 + jnp.exp(-jnp.abs(x)))


def _iota(shape, axis):
    return lax.broadcasted_iota(jnp.int32, shape, axis)


def _ones_where(cond):
    return jnp.where(cond, 1.0, 0.0).astype(BF16)


def _group_sum(x, group):
    shift = group.bit_length() - 1
    gi = lax.shift_right_logical(_iota((LANES, LANES), 0), shift)
    gj = lax.shift_right_logical(_iota((LANES, LANES), 1), shift)
    bd = _ones_where(gi == gj)
    parts = _split_bf16(x, 3)
    outs = []
    for c in range(x.shape[-1] // LANES):
        sl = slice(c * LANES, (c + 1) * LANES)
        acc = None
        for p in parts:
            t = jnp.dot(p[:, sl], bd, preferred_element_type=F32)
            acc = t if acc is None else acc + t
        outs.append(acc)
    return outs[0] if len(outs) == 1 else jnp.concatenate(outs, axis=-1)


def _chunk_masks(tb):
    shift = CHUNK.bit_length() - 1
    i = _iota((tb, tb), 0)
    j = _iota((tb, tb), 1)
    same = lax.shift_right_logical(i, shift) == lax.shift_right_logical(j, shift)
    return _ones_where(same & (i >= j)), _ones_where(same)


def _shift_rows(carry, x, j):
    xx = jnp.concatenate([carry, x], axis=0)
    return pltpu.roll(xx, j, 0)[SUBLANES:]


def _unit_lower_inverse(a, prec):
    n = a.shape[0]
    eye = jnp.where(_iota((n, n), 0) == _iota((n, n), 1), 1.0, 0.0)
    inv = eye + a
    pw = a
    steps = CHUNK.bit_length() - 2
    for _ in range(steps):
        pw = _mm(pw, pw, "nn", prec, prec)
        inv = inv + _mm(inv, pw, "nn", prec, prec)
    return inv


def _ffn_kernel(x_ref, g_ref, wg_ref, wu_ref, wd_ref, g2_ref, o_ref, *rest, nj, mode):
    if mode == "emit":
        u_ref, xn_ref, acc_ref = rest
    else:
        xn_ref, acc_ref = rest
    j = pl.program_id(1)

    @pl.when(j == 0)
    def _():
        xn_ref[...] = _rms(x_ref[...], g_ref[...]).astype(BF16)
        acc_ref[...] = jnp.zeros_like(acc_ref)

    xn = xn_ref[...]
    gate = jnp.dot(xn, wg_ref[...], preferred_element_type=F32)
    up = jnp.dot(xn, wu_ref[...], preferred_element_type=F32)
    mid = (gate * _sigmoid(gate) * up).astype(BF16)
    acc_ref[...] += jnp.dot(mid, wd_ref[...], preferred_element_type=F32)

    @pl.when(j == nj - 1)
    def _():
        h = x_ref[...] + 0.5 * acc_ref[...]
        if mode == "emit":
            o_ref[...] = h
            u_ref[...] = _rms(h, g2_ref[...]).astype(BF16)
        else:
            o_ref[...] = _rms(h, g2_ref[...])


def _ffn(x, g, wg, wu, wd, g2, mode, tm=512, tf=512):
    t, d = x.shape
    f = wg.shape[1]
    nj = f // tf
    out_shape = [jax.ShapeDtypeStruct((t, d), F32)]
    out_specs = [pl.BlockSpec((tm, d), lambda i, j: (i, 0))]
    if mode == "emit":
        out_shape.append(jax.ShapeDtypeStruct((t, d), BF16))
        out_specs.append(pl.BlockSpec((tm, d), lambda i, j: (i, 0)))
    res = pl.pallas_call(
        functools.partial(_ffn_kernel, nj=nj, mode=mode),
        grid=(t // tm, nj),
        in_specs=[
            pl.BlockSpec((tm, d), lambda i, j: (i, 0)),
            pl.BlockSpec((1, d), lambda i, j: (0, 0)),
            pl.BlockSpec((d, tf), lambda i, j: (0, j)),
            pl.BlockSpec((d, tf), lambda i, j: (0, j)),
            pl.BlockSpec((tf, d), lambda i, j: (j, 0)),
            pl.BlockSpec((1, d), lambda i, j: (0, 0)),
        ],
        out_specs=out_specs,
        out_shape=out_shape,
        scratch_shapes=[pltpu.VMEM((tm, d), BF16), pltpu.VMEM((tm, d), F32)],
        compiler_params=pltpu.CompilerParams(
            dimension_semantics=("arbitrary", "arbitrary"), vmem_limit_bytes=VMEM_LIMIT),
        name="ffn_" + mode,
    )(x, g, wg, wu, wd, g2)
    return res


def _proj_kernel(x_ref, w_ref, o_ref):
    o_ref[...] = jnp.dot(x_ref[...], w_ref[...], preferred_element_type=F32).astype(o_ref.dtype)


def _proj(x, w, tn, name, tm=1024, out_dtype=F32):
    t, k = x.shape
    n = w.shape[1]
    tm = min(tm, t)
    return pl.pallas_call(
        _proj_kernel,
        grid=(n // tn, t // tm),
        in_specs=[
            pl.BlockSpec((tm, k), lambda c, i: (i, 0)),
            pl.BlockSpec((k, tn), lambda c, i: (0, c)),
        ],
        out_specs=pl.BlockSpec((tm, tn), lambda c, i: (i, c)),
        out_shape=jax.ShapeDtypeStruct((t, n), out_dtype),
        compiler_params=pltpu.CompilerParams(
            dimension_semantics=("arbitrary", "arbitrary"), vmem_limit_bytes=VMEM_LIMIT),
        name=name,
    )(x, w)


RW_PREC = 2
RW_LORA_PREC = 2


def _rwkv_kernel(p_ref, mu_ref, w0_ref, w2_ref, a0_ref, a2_ref, g2_ref, kk_ref, ka_ref, rk_ref,
                 lnw_ref, lnb_ref, o_ref,
                 carry_ref, st_ref, at_s, rt_s, bt_s, kt_s, bh_s, kh_s, v_s, wl_s, y_s, bonus_s, g_s,
                 *, tb):
    @pl.when(pl.program_id(0) == 0)
    def _():
        carry_ref[...] = jnp.zeros_like(carry_ref)
        st_ref[...] = jnp.zeros_like(st_ref)

    cols = p_ref[...]
    prev = _shift_rows(carry_ref[...], cols, 1)
    carry_ref[...] = cols[tb - SUBLANES:]
    x = cols + (prev - cols) * mu_ref[...]

    r = x[:, 0:RW_WIDTH]
    k = x[:, RW_WIDTH:2 * RW_WIDTH]
    v = x[:, 2 * RW_WIDTH:3 * RW_WIDTH]
    wl = x[:, RW_OFF_WL:RW_OFF_AL]
    al = x[:, RW_OFF_AL:RW_OFF_GL]
    gl = x[:, RW_OFF_GL:RW_PAD_COLS]

    lp = RW_LORA_PREC
    zw = w0_ref[...] + _mm(jnp.tanh(wl), w2_ref[...], "nn", lp, lp)
    w_log = -_softplus(-zw) - 0.5
    lw = -jnp.exp(w_log)
    a = _sigmoid(a0_ref[...] + _mm(al, a2_ref[...], "nn", lp, lp))
    g_s[...] = _mm(_sigmoid(gl), g2_ref[...], "nn", lp, lp)

    kkv = k * kk_ref[...]
    kkn = kkv * lax.rsqrt(_group_sum(kkv * kkv, RW_HEAD_DIM) + 1e-6)
    k2 = k * (1.0 + (a - 1.0) * ka_ref[...])
    aa = -kkn
    bb = kkn * a
    bonus_s[...] = _group_sum(r * k2 * rk_ref[...], RW_HEAD_DIM) * v

    tri, tot = _chunk_masks(tb)
    c = _mm(tri, lw, "nn", 1, 3)
    cl = _mm(tot, lw, "nn", 1, 3)
    einv = jnp.exp(-c)
    ehat = jnp.exp(cl - c)
    rt_s[...] = r * jnp.exp(c)
    at_s[...] = aa * jnp.exp(c - lw)
    bt_s[...] = bb * einv
    kt_s[...] = k2 * einv
    bh_s[...] = bb * ehat
    kh_s[...] = k2 * ehat
    wl_s[...] = jnp.exp(cl)
    v_s[...] = v

    L = CHUNK
    P2 = 2 * L
    pr = RW_PREC
    lane_l = _iota((L, P2), 1)
    row_l = _iota((L, P2), 0)
    h0_l = lane_l < L
    strict_l = row_l > (lane_l & (L - 1))
    incl_l = row_l >= (lane_l & (L - 1))
    rr = _iota((P2, P2), 0)
    cc = _iota((P2, P2), 1)
    same_blk = (rr >= L) == (cc >= L)
    strict_bd = same_blk & ((rr & (L - 1)) > (cc & (L - 1)))
    top = rr < L

    def chunk(ci, carry):
        rows = pl.ds(pl.multiple_of(ci * L, L), L)
        for p in range(RW_HEADS // 2):
            ls = slice(p * P2, (p + 1) * P2)
            at = at_s[rows, ls]
            rt = rt_s[rows, ls]
            bt = bt_s[rows, ls]
            kt = kt_s[rows, ls]
            vv = v_s[rows, ls]
            s0 = st_ref[p]
            zero = jnp.zeros_like(at)
            g0 = _mm(jnp.concatenate([jnp.where(h0_l, at, zero), jnp.where(h0_l, rt, zero)], 0),
                     jnp.concatenate([bt, kt], 0), "nt", pr, pr)
            g1 = _mm(jnp.concatenate([jnp.where(h0_l, zero, rt), jnp.where(h0_l, zero, at)], 0),
                     jnp.concatenate([kt, bt], 0), "nt", pr, pr)
            a_bd = jnp.where(strict_bd, jnp.where(top, g0, g1), 0.0)
            inv = _unit_lower_inverse(a_bd, pr)
            xs = _mm(jnp.concatenate([at, rt], 0), s0, "nt", pr, pr)
            m_ak = jnp.where(strict_l, jnp.where(h0_l, g1[L:], g0[:L]), 0.0)
            v0 = jnp.where(h0_l, vv, zero)
            v1 = jnp.where(h0_l, zero, vv)
            rhs = xs[:L] + _mm(m_ak, jnp.concatenate([v1, v0], 0), "nn", pr, pr)
            pp = _mm(inv, jnp.concatenate([jnp.where(h0_l, rhs, zero), jnp.where(h0_l, zero, rhs)], 0),
                     "nn", pr, pr)
            pm = pp[:L] + pp[L:]
            p0 = jnp.where(h0_l, pm, zero)
            p1 = jnp.where(h0_l, zero, pm)
            ly = jnp.concatenate([jnp.where(incl_l, g0[L:], 0.0), jnp.where(incl_l, g1[:L], 0.0)], 1)
            y = xs[L:] + _mm(ly, jnp.concatenate([p0, v0, v1, p1], 0), "nn", pr, pr)
            y_s[rows, ls] = y
            upd = _mm(jnp.concatenate([pm, vv], 0),
                      jnp.concatenate([bh_s[rows, ls], kh_s[rows, ls]], 0), "tn", pr, pr)
            wl_row = wl_s[pl.ds(pl.multiple_of(ci * L, L), 1), ls]
            st_ref[p] = s0 * wl_row + jnp.where(same_blk, upd, 0.0)
        return carry

    lax.fori_loop(0, tb // L, chunk, 0)

    y = y_s[...]
    inv_n = 1.0 / RW_HEAD_DIM
    mu = _group_sum(y, RW_HEAD_DIM) * inv_n
    yc = y - mu
    var = _group_sum(yc * yc, RW_HEAD_DIM) * inv_n
    yn = yc * lax.rsqrt(var + RW_GN_EPS) * lnw_ref[...] + lnb_ref[...]
    o_ref[...] = ((yn + bonus_s[...]) * g_s[...]).astype(o_ref.dtype)


def _rwkv(p_rw, mu, w0, w2, a0, a2, g2, k_k, k_a, r_k, ln_w, ln_b, tb=256):
    t = p_rw.shape[0]
    tb = min(tb, t)
    w = RW_WIDTH

    def full(a):
        return pl.BlockSpec(a.shape, lambda i: (0,) * a.ndim)

    params = (mu, w0, w2, a0, a2, g2, k_k, k_a, r_k, ln_w, ln_b)
    big = pltpu.VMEM((tb, w), F32)
    return pl.pallas_call(
        functools.partial(_rwkv_kernel, tb=tb),
        grid=(t // tb,),
        in_specs=[pl.BlockSpec((tb, RW_PAD_COLS), lambda i: (i, 0))] + [full(a) for a in params],
        out_specs=pl.BlockSpec((tb, w), lambda i: (i, 0)),
        out_shape=jax.ShapeDtypeStruct((t, w), BF16),
        scratch_shapes=[pltpu.VMEM((SUBLANES, RW_PAD_COLS), F32),
                        pltpu.VMEM((RW_HEADS // 2, 2 * CHUNK, 2 * CHUNK), F32)] + [big] * 11,
        compiler_params=pltpu.CompilerParams(
            dimension_semantics=("arbitrary",), vmem_limit_bytes=VMEM_LIMIT),
        name="rwkv7",
    )(p_rw, *params)


GDN_PREC = 2


def _gdn_kernel(p_ref, cw_ref, alog_ref, dtb_ref, nw_ref, o_ref,
                carry_ref, st_ref, q_s, k_s, vb_s, kbe_s, qe_s, kh_s, bx_s, gcx_s, egl_s, o_s, *, tb):
    @pl.when(pl.program_id(0) == 0)
    def _():
        carry_ref[...] = jnp.zeros_like(carry_ref)
        st_ref[...] = jnp.zeros_like(st_ref)

    xin = p_ref[:, 0:GDN_CONV_CH]
    cw = cw_ref[...]
    carry = carry_ref[...]
    acc = xin * cw[GDN_CONV - 1:GDN_CONV]
    for j in range(1, GDN_CONV):
        acc = acc + _shift_rows(carry, xin, j) * cw[GDN_CONV - 1 - j:GDN_CONV - j]
    carry_ref[...] = xin[tb - SUBLANES:]
    qkv = acc * _sigmoid(acc)

    def l2n(z):
        return z * lax.rsqrt(jnp.sum(z * z, axis=-1, keepdims=True) + 1e-6)

    hd = GDN_HEAD_DIM
    for h in range(GDN_QK_HEADS):
        q_s[:, h * hd:(h + 1) * hd] = l2n(qkv[:, h * hd:(h + 1) * hd]) * (hd ** -0.5)
        k_s[:, h * hd:(h + 1) * hd] = l2n(qkv[:, GDN_QK_WIDTH + h * hd:GDN_QK_WIDTH + (h + 1) * hd])
    v = qkv[:, 2 * GDN_QK_WIDTH:]

    ba = p_ref[:, GDN_OFF_BA:GDN_PAD_COLS]
    beta = _sigmoid(ba)
    gstep = -jnp.exp(alog_ref[...]) * _softplus(ba + dtb_ref[...])
    tri, tot = _chunk_masks(tb)
    gc = _mm(tri, gstep, "nn", 1, 3)
    gl = _mm(tot, gstep, "nn", 1, 3)
    hh = _iota((LANES, GDN_V_WIDTH), 0)
    blk = lax.shift_right_logical(_iota((LANES, GDN_V_WIDTH), 1), hd.bit_length() - 1)
    e_beta = _ones_where(hh == blk)
    e_g = _ones_where(hh == blk + GDN_V_HEADS)
    bx = _mm(beta, e_beta, "nn", 3, 1)
    gcx = _mm(gc, e_g, "nn", 3, 1)
    glx = _mm(gl, e_g, "nn", 3, 1)
    bx_s[...] = bx
    gcx_s[...] = gcx
    egl_s[...] = jnp.exp(glx)
    egc = jnp.exp(gcx)
    ehat = jnp.exp(glx - gcx)
    vb_s[...] = v * bx
    rep = GDN_V_HEADS // GDN_QK_HEADS
    for h in range(GDN_V_HEADS):
        sl = slice(h * hd, (h + 1) * hd)
        qs = slice((h // rep) * hd, (h // rep + 1) * hd)
        kh = k_s[:, qs]
        kbe_s[:, sl] = kh * bx[:, sl] * egc[:, sl]
        qe_s[:, sl] = q_s[:, qs] * egc[:, sl]
        kh_s[:, sl] = kh * ehat[:, sl]

    L = CHUNK
    P2 = 2 * L
    pr = GDN_PREC
    rr = _iota((P2, P2), 0)
    cc = _iota((P2, P2), 1)
    same_blk = (rr >= L) == (cc >= L)
    lr = rr & (L - 1)
    lc = cc & (L - 1)
    strict_bd = same_blk & (lr > lc)
    incl_bd = same_blk & (lr >= lc)

    def chunk(ci, carry_):
        r0 = pl.multiple_of(ci * L, L)
        rows = pl.ds(r0, L)
        for p in range(GDN_QK_HEADS):
            qs = slice(p * hd, (p + 1) * hd)
            h0 = slice((2 * p) * hd, (2 * p + 1) * hd)
            h1 = slice((2 * p + 1) * hd, (2 * p + 2) * hd)
            kp = k_s[rows, qs]
            qp = q_s[rows, qs]
            kk2 = jnp.concatenate([kp, kp], 0)
            gm = _mm(jnp.concatenate([kp, qp], 0), kk2, "nt", pr, pr)
            gcol = jnp.concatenate([gcx_s[rows, h0], gcx_s[rows, h1]], 0)
            diff = gcol - gcol.T
            decay = jnp.where(incl_bd, jnp.exp(jnp.where(incl_bd, diff, 0.0)), 0.0)
            bcol = jnp.concatenate([bx_s[rows, h0], bx_s[rows, h1]], 0)
            a_bd = jnp.where(strict_bd, -(jnp.concatenate([gm[:L], gm[:L]], 0) * bcol * decay), 0.0)
            inv = _unit_lower_inverse(a_bd, pr)
            a_qk = jnp.concatenate([gm[L:], gm[L:]], 0) * decay
            rhs = jnp.concatenate([jnp.concatenate([vb_s[rows, h0], kbe_s[rows, h0]], 1),
                                   jnp.concatenate([vb_s[rows, h1], kbe_s[rows, h1]], 1)], 0)
            uw = _mm(inv, rhs, "nn", pr, pr)
            vnew = []
            xs = []
            for hi, hs in enumerate((h0, h1)):
                s0 = st_ref[2 * p + hi]
                u_h = uw[hi * L:(hi + 1) * L, :hd]
                w_h = uw[hi * L:(hi + 1) * L, hd:]
                x_h = _mm(jnp.concatenate([w_h, qe_s[rows, hs]], 0), s0, "nn", pr, pr)
                vnew.append(u_h - x_h[:L])
                xs.append(x_h[L:])
            o_pair = _mm(a_qk, jnp.concatenate(vnew, 0), "nn", pr, pr)
            for hi, hs in enumerate((h0, h1)):
                o_s[rows, hs] = xs[hi] + o_pair[hi * L:(hi + 1) * L]
                s0 = st_ref[2 * p + hi]
                upd = _mm(kh_s[rows, hs], vnew[hi], "tn", pr, pr)
                st_ref[2 * p + hi] = s0 * egl_s[pl.ds(r0, 1), hs] + upd
        return carry_

    lax.fori_loop(0, tb // L, chunk, 0)

    z = p_ref[:, GDN_OFF_Z:GDN_OFF_BA]
    nw = nw_ref[...]
    for h in range(GDN_V_HEADS):
        sl = slice(h * hd, (h + 1) * hd)
        o = o_s[:, sl]
        zz = z[:, sl]
        on = o * lax.rsqrt(jnp.mean(o * o, axis=-1, keepdims=True) + RMS_EPS) * nw
        o_ref[:, sl] = (on * (zz * _sigmoid(zz))).astype(o_ref.dtype)


def _gdn(p_gdn, conv_w, alog_pad, dtb_pad, norm_w, tb=256):
    t = p_gdn.shape[0]
    tb = min(tb, t)

    def full(a):
        return pl.BlockSpec(a.shape, lambda i: (0,) * a.ndim)

    params = (conv_w, alog_pad, dtb_pad, norm_w)
    qk = pltpu.VMEM((tb, GDN_QK_WIDTH), F32)
    vw = pltpu.VMEM((tb, GDN_V_WIDTH), F32)
    return pl.pallas_call(
        functools.partial(_gdn_kernel, tb=tb),
        grid=(t // tb,),
        in_specs=[pl.BlockSpec((tb, GDN_PAD_COLS), lambda i: (i, 0))] + [full(a) for a in params],
        out_specs=pl.BlockSpec((tb, GDN_V_WIDTH), lambda i: (i, 0)),
        out_shape=jax.ShapeDtypeStruct((t, GDN_V_WIDTH), BF16),
        scratch_shapes=[pltpu.VMEM((SUBLANES, GDN_CONV_CH), F32),
                        pltpu.VMEM((GDN_V_HEADS, GDN_HEAD_DIM, GDN_HEAD_DIM), F32),
                        qk, qk] + [vw] * 8,
        compiler_params=pltpu.CompilerParams(
            dimension_semantics=("arbitrary",), vmem_limit_bytes=VMEM_LIMIT),
        name="gdn",
    )(p_gdn, *params)


def _merge_kernel(u_ref, ya_ref, yb_ref, wga_ref, wgb_ref, la_ref, lb_ref, o_ref):
    u = u_ref[...]
    ga = _sigmoid(jnp.dot(u, wga_ref[...], preferred_element_type=F32))
    gb = _sigmoid(jnp.dot(u, wgb_ref[...], preferred_element_type=F32))
    ma = jnp.dot(ya_ref[...], la_ref[...], preferred_element_type=F32)
    mb = jnp.dot(yb_ref[...], lb_ref[...], preferred_element_type=F32)
    o_ref[...] = (ga * ma + gb * mb).astype(o_ref.dtype)


def _merge(u, ya, yb, wga, wgb, la, lb, tm=1024, tn=512):
    t, d = u.shape
    tm = min(tm, t)
    bw = ya.shape[1]
    return pl.pallas_call(
        _merge_kernel,
        grid=(d // tn, t // tm),
        in_specs=[
            pl.BlockSpec((tm, d), lambda c, i: (i, 0)),
            pl.BlockSpec((tm, bw), lambda c, i: (i, 0)),
            pl.BlockSpec((tm, bw), lambda c, i: (i, 0)),
            pl.BlockSpec((d, tn), lambda c, i: (0, c)),
            pl.BlockSpec((d, tn), lambda c, i: (0, c)),
            pl.BlockSpec((bw, tn), lambda c, i: (0, c)),
            pl.BlockSpec((bw, tn), lambda c, i: (0, c)),
        ],
        out_specs=pl.BlockSpec((tm, tn), lambda c, i: (i, c)),
        out_shape=jax.ShapeDtypeStruct((t, d), BF16),
        compiler_params=pltpu.CompilerParams(
            dimension_semantics=("arbitrary", "arbitrary"), vmem_limit_bytes=VMEM_LIMIT),
        name="merge",
    )(u, ya, yb, wga, wgb, la, lb)


def _resid_kernel(x_ref, w_ref, r_ref, g_ref, h_ref, *rest, emit):
    h = r_ref[...] + jnp.dot(x_ref[...], w_ref[...], preferred_element_type=F32)
    h_ref[...] = h
    if emit:
        rest[0][...] = _rms(h, g_ref[...]).astype(BF16)


def _resid_proj(x, w, resid, g, emit, name, tm=512):
    t, k = x.shape
    d = w.shape[1]
    tm = min(tm, t)
    out_shape = [jax.ShapeDtypeStruct((t, d), F32)]
    out_specs = [pl.BlockSpec((tm, d), lambda i: (i, 0))]
    if emit:
        out_shape.append(jax.ShapeDtypeStruct((t, d), BF16))
        out_specs.append(pl.BlockSpec((tm, d), lambda i: (i, 0)))
    return pl.pallas_call(
        functools.partial(_resid_kernel, emit=emit),
        grid=(t // tm,),
        in_specs=[
            pl.BlockSpec((tm, k), lambda i: (i, 0)),
            pl.BlockSpec((k, d), lambda i: (0, 0)),
            pl.BlockSpec((tm, d), lambda i: (i, 0)),
            pl.BlockSpec((1, d), lambda i: (0, 0)),
        ],
        out_specs=out_specs,
        out_shape=out_shape,
        compiler_params=pltpu.CompilerParams(
            dimension_semantics=("arbitrary",), vmem_limit_bytes=VMEM_LIMIT),
        name=name,
    )(x, w, resid, g)


def _memkv_kernel(m_ref, g_ref, wk_ref, wv_ref, k_ref, v_ref):
    mn = _rms(m_ref[...], g_ref[...]).astype(BF16)
    k_ref[...] = jnp.dot(mn, wk_ref[...], preferred_element_type=F32).astype(BF16)
    v_ref[...] = jnp.dot(mn, wv_ref[...], preferred_element_type=F32).astype(BF16)


def _memkv(mem, g, wk, wv, tn=512):
    m, d = mem.shape
    return pl.pallas_call(
        _memkv_kernel,
        grid=(d // tn,),
        in_specs=[
            pl.BlockSpec((m, d), lambda c: (0, 0)),
            pl.BlockSpec((1, d), lambda c: (0, 0)),
            pl.BlockSpec((d, tn), lambda c: (0, c)),
            pl.BlockSpec((d, tn), lambda c: (0, c)),
        ],
        out_specs=[pl.BlockSpec((m, tn), lambda c: (0, c))] * 2,
        out_shape=[jax.ShapeDtypeStruct((m, d), BF16)] * 2,
        compiler_params=pltpu.CompilerParams(
            dimension_semantics=("arbitrary",), vmem_limit_bytes=VMEM_LIMIT),
        name="memkv",
    )(mem, g, wk, wv)


def _xattn_kernel(hq_ref, wq_ref, k_ref, v_ref, o_ref):
    q = jnp.dot(hq_ref[...], wq_ref[...], preferred_element_type=F32)
    hd = MEM_HEAD_DIM
    for h in range(MEM_HEADS):
        sl = slice(h * hd, (h + 1) * hd)
        s = _mm(q[:, sl].astype(BF16), k_ref[:, sl], "nt") * (hd ** -0.5)
        e = jnp.exp(s - jnp.max(s, axis=-1, keepdims=True))
        p = e / jnp.sum(e, axis=-1, keepdims=True)
        o_ref[:, sl] = jnp.dot(p.astype(BF16), v_ref[:, sl], preferred_element_type=F32).astype(BF16)


def _xattn(hq, wq, k, v, tm=512):
    t, d = hq.shape
    tm = min(tm, t)
    m = k.shape[0]
    return pl.pallas_call(
        _xattn_kernel,
        grid=(t // tm,),
        in_specs=[
            pl.BlockSpec((tm, d), lambda i: (i, 0)),
            pl.BlockSpec((d, d), lambda i: (0, 0)),
            pl.BlockSpec((m, d), lambda i: (0, 0)),
            pl.BlockSpec((m, d), lambda i: (0, 0)),
        ],
        out_specs=pl.BlockSpec((tm, d), lambda i: (i, 0)),
        out_shape=jax.ShapeDtypeStruct((t, d), BF16),
        compiler_params=pltpu.CompilerParams(
            dimension_semantics=("arbitrary",), vmem_limit_bytes=VMEM_LIMIT),
        name="xattn",
    )(hq, wq, k, v)


def _pad_cols(a, n):
    return jnp.pad(a, ((0, 0), (0, n - a.shape[1])))


def _pad_rows(a, n):
    return jnp.pad(a, ((0, n - a.shape[0]), (0, 0)))


def _rw_layout(a):
    o1 = 3 * RW_WIDTH
    o2 = o1 + RW_DECAY_LORA
    o3 = o2 + RW_AAA_LORA
    return jnp.concatenate([
        a[:, :o1],
        _pad_cols(a[:, o1:o2], LANES),
        _pad_cols(a[:, o2:o3], LANES),
        _pad_cols(a[:, o3:], 2 * LANES),
    ], axis=1)


def kernel(x, mem, ffn1_norm, ffn1_w_gate, ffn1_w_up, ffn1_w_down, mix_norm, w_in, rw_shift_mu, rw_w0, rw_w2, rw_a0, rw_a2, rw_g2, rw_k_k, rw_k_a, rw_r_k, rw_ln_w, rw_ln_b, gdn_conv_w, gdn_a_log, gdn_dt_bias, gdn_norm_w, w_lift, w_out, mem_q_norm, mem_kv_norm, w_mq, w_mk, w_mv, w_mo, ffn2_norm, ffn2_w_gate, ffn2_w_up, ffn2_w_down, final_norm):
    b_, t, d = x.shape
    assert b_ == 1 and d == D_MODEL and ffn1_norm.shape[0] == 1
    l = 0
    bf = lambda a: a.astype(BF16)
    row = lambda a: a.reshape(1, -1)

    h0 = x.reshape(t, d)
    h1, u = _ffn(h0, row(ffn1_norm[l]), bf(ffn1_w_gate[l]), bf(ffn1_w_up[l]), bf(ffn1_w_down[l]),
                 row(mix_norm[l]), "emit")

    w_in_l = w_in[l]
    w_rw = bf(_rw_layout(w_in_l[:, :RW_COLS]))
    w_gdn = bf(_pad_cols(w_in_l[:, RW_COLS:RW_COLS + GDN_COLS], GDN_PAD_COLS))
    w_gate = w_in_l[:, RW_COLS + GDN_COLS:]
    wga = bf(w_gate[:, :D_MODEL])
    wgb = bf(w_gate[:, D_MODEL:])

    p_rw = _proj(u, w_rw, 512, "proj_rw")
    p_gdn = _proj(u, w_gdn, 640, "proj_gdn")

    y_rw = _rwkv(
        p_rw, _rw_layout(row(rw_shift_mu[l])), row(rw_w0[l]), _pad_rows(rw_w2[l], LANES),
        row(rw_a0[l]), _pad_rows(rw_a2[l], LANES), _pad_rows(rw_g2[l], 2 * LANES),
        row(rw_k_k[l]), row(rw_k_a[l]), row(rw_r_k[l]), row(rw_ln_w[l]), row(rw_ln_b[l]))

    nh = GDN_V_HEADS
    alog_pad = jnp.pad(row(gdn_a_log[l]), ((0, 0), (nh, LANES - 2 * nh)))
    dtb_pad = jnp.pad(row(gdn_dt_bias[l]), ((0, 0), (nh, LANES - 2 * nh)))
    y_gdn = _gdn(p_gdn, gdn_conv_w[l], alog_pad, dtb_pad, row(gdn_norm_w[l]))

    merged = _merge(u, y_rw, y_gdn, wga, wgb, bf(w_lift[l, 0]), bf(w_lift[l, 1]))
    h2, hq = _resid_proj(merged, bf(w_out[l]), h1, row(mem_q_norm[l]), True, "out_proj")

    mk, mv = _memkv(mem.reshape(mem.shape[1], d), row(mem_kv_norm[l]), bf(w_mk[l]), bf(w_mv[l]))
    att = _xattn(hq, bf(w_mq[l]), mk, mv)
    (h3,) = _resid_proj(att, bf(w_mo[l]), h2, row(mem_q_norm[l]), False, "attn_out")

    (y,) = _ffn(h3, row(ffn2_norm[l]), bf(ffn2_w_gate[l]), bf(ffn2_w_up[l]), bf(ffn2_w_down[l]),
                row(final_norm), "final")
    return y.reshape(b_, t, d)
```

```python
import functools

import jax
import jax.numpy as jnp
from jax import lax
from jax.experimental import pallas as pl
from jax.experimental.pallas import tpu as pltpu

F32 = jnp.float32
BF16 = jnp.bfloat16

D_MODEL = 2048
D_FF = 5632
RMS_EPS = 1e-6

RW_HEADS = 16
RW_HEAD_DIM = 64
RW_WIDTH = 1024
RW_DECAY_LORA = 64
RW_AAA_LORA = 64
RW_GATE_LORA = 160
RW_GN_EPS = 64e-5
RW_COLS = 3 * RW_WIDTH + RW_DECAY_LORA + RW_AAA_LORA + RW_GATE_LORA
LANES = 128
RW_OFF_WL = 3 * RW_WIDTH
RW_OFF_AL = RW_OFF_WL + LANES
RW_OFF_GL = RW_OFF_AL + LANES
RW_PAD_COLS = RW_OFF_GL + 2 * LANES

GDN_QK_HEADS = 4
GDN_V_HEADS = 8
GDN_HEAD_DIM = 128
GDN_QK_WIDTH = 512
GDN_V_WIDTH = 1024
GDN_CONV = 4
GDN_CONV_CH = 2 * GDN_QK_WIDTH + GDN_V_WIDTH
GDN_COLS = GDN_CONV_CH + GDN_V_WIDTH + 2 * GDN_V_HEADS
GDN_OFF_Z = GDN_CONV_CH
GDN_OFF_BA = GDN_CONV_CH + GDN_V_WIDTH
GDN_PAD_COLS = GDN_OFF_BA + LANES

MEM_HEADS = 4
MEM_HEAD_DIM = D_MODEL // MEM_HEADS

CHUNK = 64
SUBLANES = 8
VMEM_LIMIT = 56 * 1024 * 1024

_DIMS = {
    "nn": (((1,), (0,)), ((), ())),
    "nt": (((1,), (1,)), ((), ())),
    "tn": (((0,), (0,)), ((), ())),
}


def _split_bf16(x, n):
    if x.dtype == BF16:
        return [x]
    parts = []
    r = x
    for i in range(n):
        p = r.astype(BF16)
        parts.append(p)
        if i + 1 < n:
            r = r - p.astype(F32)
    return parts


def _mm(a, b, dims="nn", pa=1, pb=1):
    ap = _split_bf16(a, pa)
    bp = _split_bf16(b, pb)
    n = max(len(ap), len(bp))
    acc = None
    for i, ai in enumerate(ap):
        for j, bj in enumerate(bp):
            if i + j < n:
                t = lax.dot_general(ai, bj, _DIMS[dims], preferred_element_type=F32)
                acc = t if acc is None else acc + t
    return acc


def _rms(x, g, eps=RMS_EPS):
    ms = jnp.mean(x * x, axis=-1, keepdims=True)
    return x * lax.rsqrt(ms + eps) * g


def _sigmoid(x):
    return 1.0 / (1.0 + jnp.exp(-x))


def _softplus(x):
    return jnp.maximum(x, 0.0) + jnp.log(1.0 + jnp.exp(-jnp.abs(x)))


def _iota(shape, axis):
    return lax.broadcasted_iota(jnp.int32, shape, axis)


def _ones_where(cond):
    return jnp.where(cond, 1.0, 0.0).astype(BF16)


def _group_sum(x, group):
    shift = group.bit_length() - 1
    gi = lax.shift_right_logical(_iota((LANES, LANES), 0), shift)
    gj = lax.shift_right_logical(_iota((LANES, LANES), 1), shift)
    bd = _ones_where(gi == gj)
    parts = _split_bf16(x, 3)
    outs = []
    for c in range(x.shape[-1] // LANES):
        sl = slice(c * LANES, (c + 1) * LANES)
        acc = None
        for p in parts:
            t = jnp.dot(p[:, sl], bd, preferred_element_type=F32)
            acc = t if acc is None else acc + t
        outs.append(acc)
    return outs[0] if len(outs) == 1 else jnp.concatenate(outs, axis=-1)


def _chunk_masks(tb):
    shift = CHUNK.bit_length() - 1
    i = _iota((tb, tb), 0)
    j = _iota((tb, tb), 1)
    same = lax.shift_right_logical(i, shift) == lax.shift_right_logical(j, shift)
    return _ones_where(same & (i >= j)), _ones_where(same)


def _shift_rows(carry, x, j):
    xx = jnp.concatenate([carry, x], axis=0)
    return pltpu.roll(xx, j, 0)[SUBLANES:]


def _unit_lower_inverse(a, prec):
    n = a.shape[0]
    eye = jnp.where(_iota((n, n), 0) == _iota((n, n), 1), 1.0, 0.0)
    inv = eye + a
    pw = a
    steps = CHUNK.bit_length() - 2
    for _ in range(steps):
        pw = _mm(pw, pw, "nn", prec, prec)
        inv = inv + _mm(inv, pw, "nn", prec, prec)
    return inv


def _ffn_kernel(x_ref, g_ref, wg_ref, wu_ref, wd_ref, g2_ref, o_ref, *rest, nj, mode):
    if mode == "emit":
        u_ref, xn_ref, acc_ref = rest
    else:
        xn_ref, acc_ref = rest
    j = pl.program_id(1)

    @pl.when(j == 0)
    def _():
        xn_ref[...] = _rms(x_ref[...], g_ref[...]).astype(BF16)
        acc_ref[...] = jnp.zeros_like(acc_ref)

    xn = xn_ref[...]
    gate = jnp.dot(xn, wg_ref[...], preferred_element_type=F32)
    up = jnp.dot(xn, wu_ref[...], preferred_element_type=F32)
    mid = (gate * _sigmoid(gate) * up).astype(BF16)
    acc_ref[...] += jnp.dot(mid, wd_ref[...], preferred_element_type=F32)

    @pl.when(j == nj - 1)
    def _():
        h = x_ref[...] + 0.5 * acc_ref[...]
        if mode == "emit":
            o_ref[...] = h
            u_ref[...] = _rms(h, g2_ref[...]).astype(BF16)
        else:
            o_ref[...] = _rms(h, g2_ref[...])


def _ffn(x, g, wg, wu, wd, g2, mode, tm=512, tf=512):
    t, d = x.shape
    f = wg.shape[1]
    nj = f // tf
    out_shape = [jax.ShapeDtypeStruct((t, d), F32)]
    out_specs = [pl.BlockSpec((tm, d), lambda i, j: (i, 0))]
    if mode == "emit":
        out_shape.append(jax.ShapeDtypeStruct((t, d), BF16))
        out_specs.append(pl.BlockSpec((tm, d), lambda i, j: (i, 0)))
    res = pl.pallas_call(
        functools.partial(_ffn_kernel, nj=nj, mode=mode),
        grid=(t // tm, nj),
        in_specs=[
            pl.BlockSpec((tm, d), lambda i, j: (i, 0)),
            pl.BlockSpec((1, d), lambda i, j: (0, 0)),
            pl.BlockSpec((d, tf), lambda i, j: (0, j)),
            pl.BlockSpec((d, tf), lambda i, j: (0, j)),
            pl.BlockSpec((tf, d), lambda i, j: (j, 0)),
            pl.BlockSpec((1, d), lambda i, j: (0, 0)),
        ],
        out_specs=out_specs,
        out_shape=out_shape,
        scratch_shapes=[pltpu.VMEM((tm, d), BF16), pltpu.VMEM((tm, d), F32)],
        compiler_params=pltpu.CompilerParams(
            dimension_semantics=("arbitrary", "arbitrary"), vmem_limit_bytes=VMEM_LIMIT),
        name="ffn_" + mode,
    )(x, g, wg, wu, wd, g2)
    return res


def _proj_kernel(x_ref, w_ref, o_ref):
    o_ref[...] = jnp.dot(x_ref[...], w_ref[...], preferred_element_type=F32).astype(o_ref.dtype)


def _proj(x, w, tn, name, tm=1024, out_dtype=F32):
    t, k = x.shape
    n = w.shape[1]
    tm = min(tm, t)
    return pl.pallas_call(
        _proj_kernel,
        grid=(n // tn, t // tm),
        in_specs=[
            pl.BlockSpec((tm, k), lambda c, i: (i, 0)),
            pl.BlockSpec((k, tn), lambda c, i: (0, c)),
        ],
        out_specs=pl.BlockSpec((tm, tn), lambda c, i: (i, c)),
        out_shape=jax.ShapeDtypeStruct((t, n), out_dtype),
        compiler_params=pltpu.CompilerParams(
            dimension_semantics=("arbitrary", "arbitrary"), vmem_limit_bytes=VMEM_LIMIT),
        name=name,
    )(x, w)


RW_PREC = 1
RW_LORA_PREC = 2


def _rwkv_kernel(p_ref, mu_ref, w0_ref, w2_ref, a0_ref, a2_ref, g2_ref, kk_ref, ka_ref, rk_ref,
                 lnw_ref, lnb_ref, o_ref,
                 carry_ref, st_ref, at_s, rt_s, bt_s, kt_s, bh_s, kh_s, v_s, wl_s, y_s, bonus_s, g_s,
                 *, tb):
    @pl.when(pl.program_id(0) == 0)
    def _():
        carry_ref[...] = jnp.zeros_like(carry_ref)
        st_ref[...] = jnp.zeros_like(st_ref)

    cols = p_ref[...]
    prev = _shift_rows(carry_ref[...], cols, 1)
    carry_ref[...] = cols[tb - SUBLANES:]
    x = cols + (prev - cols) * mu_ref[...]

    r = x[:, 0:RW_WIDTH]
    k = x[:, RW_WIDTH:2 * RW_WIDTH]
    v = x[:, 2 * RW_WIDTH:3 * RW_WIDTH]
    wl = x[:, RW_OFF_WL:RW_OFF_AL]
    al = x[:, RW_OFF_AL:RW_OFF_GL]
    gl = x[:, RW_OFF_GL:RW_PAD_COLS]

    lp = RW_LORA_PREC
    zw = w0_ref[...] + _mm(jnp.tanh(wl), w2_ref[...], "nn", lp, lp)
    w_log = -_softplus(-zw) - 0.5
    lw = -jnp.exp(w_log)
    a = _sigmoid(a0_ref[...] + _mm(al, a2_ref[...], "nn", lp, lp))
    g_s[...] = _mm(_sigmoid(gl), g2_ref[...], "nn", lp, lp)

    kkv = k * kk_ref[...]
    kkn = kkv * lax.rsqrt(_group_sum(kkv * kkv, RW_HEAD_DIM) + 1e-6)
    k2 = k * (1.0 + (a - 1.0) * ka_ref[...])
    aa = -kkn
    bb = kkn * a
    bonus_s[...] = _group_sum(r * k2 * rk_ref[...], RW_HEAD_DIM) * v

    tri, tot = _chunk_masks(tb)
    c = _mm(tri, lw, "nn", 1, 3)
    cl = _mm(tot, lw, "nn", 1, 3)
    einv = jnp.exp(-c)
    ehat = jnp.exp(cl - c)
    rt_s[...] = r * jnp.exp(c)
    at_s[...] = aa * jnp.exp(c - lw)
    bt_s[...] = bb * einv
    kt_s[...] = k2 * einv
    bh_s[...] = bb * ehat
    kh_s[...] = k2 * ehat
    wl_s[...] = jnp.exp(cl)
    v_s[...] = v

    L = CHUNK
    P2 = 2 * L
    pr = RW_PREC
    lane_l = _iota((L, P2), 1)
    row_l = _iota((L, P2), 0)
    h0_l = lane_l < L
    strict_l = row_l > (lane_l & (L - 1))
    incl_l = row_l >= (lane_l & (L - 1))
    rr = _iota((P2, P2), 0)
    cc = _iota((P2, P2), 1)
    same_blk = (rr >= L) == (cc >= L)
    strict_bd = same_blk & ((rr & (L - 1)) > (cc & (L - 1)))
    top = rr < L

    def chunk(ci, carry):
        rows = pl.ds(pl.multiple_of(ci * L, L), L)
        for p in range(RW_HEADS // 2):
            ls = slice(p * P2, (p + 1) * P2)
            at = at_s[rows, ls]
            rt = rt_s[rows, ls]
            bt = bt_s[rows, ls]
            kt = kt_s[rows, ls]
            vv = v_s[rows, ls]
            s0 = st_ref[p]
            zero = jnp.zeros_like(at)
            g0 = _mm(jnp.concatenate([jnp.where(h0_l, at, zero), jnp.where(h0_l, rt, zero)], 0),
                     jnp.concatenate([bt, kt], 0), "nt", pr, pr)
            g1 = _mm(jnp.concatenate([jnp.where(h0_l, zero, rt), jnp.where(h0_l, zero, at)], 0),
                     jnp.concatenate([kt, bt], 0), "nt", pr, pr)
            a_bd = jnp.where(strict_bd, jnp.where(top, g0, g1), 0.0)
            inv = _unit_lower_inverse(a_bd, pr)
            xs = _mm(jnp.concatenate([at, rt], 0), s0, "nt", pr, pr)
            m_ak = jnp.where(strict_l, jnp.where(h0_l, g1[L:], g0[:L]), 0.0)
            v0 = jnp.where(h0_l, vv, zero)
            v1 = jnp.where(h0_l, zero, vv)
            rhs = xs[:L] + _mm(m_ak, jnp.concatenate([v1, v0], 0), "nn", pr, pr)
            pp = _mm(inv, jnp.concatenate([jnp.where(h0_l, rhs, zero), jnp.where(h0_l, zero, rhs)], 0),
                     "nn", pr, pr)
            pm = pp[:L] + pp[L:]
            p0 = jnp.where(h0_l, pm, zero)
            p1 = jnp.where(h0_l, zero, pm)
            ly = jnp.concatenate([jnp.where(incl_l, g0[L:], 0.0), jnp.where(incl_l, g1[:L], 0.0)], 1)
            y = xs[L:] + _mm(ly, jnp.concatenate([p0, v0, v1, p1], 0), "nn", pr, pr)
            y_s[rows, ls] = y
            upd = _mm(jnp.concatenate([pm, vv], 0),
                      jnp.concatenate([bh_s[rows, ls], kh_s[rows, ls]], 0), "tn", pr, pr)
            wl_row = wl_s[pl.ds(pl.multiple_of(ci * L, L), 1), ls]
            st_ref[p] = s0 * wl_row + jnp.where(same_blk, upd, 0.0)
        return carry

    lax.fori_loop(0, tb // L, chunk, 0)

    y = y_s[...]
    inv_n = 1.0 / RW_HEAD_DIM
    mu = _group_sum(y, RW_HEAD_DIM) * inv_n
    yc = y - mu
    var = _group_sum(yc * yc, RW_HEAD_DIM) * inv_n
    yn = yc * lax.rsqrt(var + RW_GN_EPS) * lnw_ref[...] + lnb_ref[...]
    o_ref[...] = ((yn + bonus_s[...]) * g_s[...]).astype(o_ref.dtype)


def _rwkv(p_rw, mu, w0, w2, a0, a2, g2, k_k, k_a, r_k, ln_w, ln_b, tb=256):
    t = p_rw.shape[0]
    tb = min(tb, t)
    w = RW_WIDTH

    def full(a):
        return pl.BlockSpec(a.shape, lambda i: (0,) * a.ndim)

    params = (mu, w0, w2, a0, a2, g2, k_k, k_a, r_k, ln_w, ln_b)
    big = pltpu.VMEM((tb, w), F32)
    return pl.pallas_call(
        functools.partial(_rwkv_kernel, tb=tb),
        grid=(t // tb,),
        in_specs=[pl.BlockSpec((tb, RW_PAD_COLS), lambda i: (i, 0))] + [full(a) for a in params],
        out_specs=pl.BlockSpec((tb, w), lambda i: (i, 0)),
        out_shape=jax.ShapeDtypeStruct((t, w), BF16),
        scratch_shapes=[pltpu.VMEM((SUBLANES, RW_PAD_COLS), F32),
                        pltpu.VMEM((RW_HEADS // 2, 2 * CHUNK, 2 * CHUNK), F32)] + [big] * 11,
        compiler_params=pltpu.CompilerParams(
            dimension_semantics=("arbitrary",), vmem_limit_bytes=VMEM_LIMIT),
        name="rwkv7",
    )(p_rw, *params)


GDN_PREC = 1


def _gdn_kernel(p_ref, cw_ref, alog_ref, dtb_ref, nw_ref, o_ref,
                carry_ref, st_ref, q_s, k_s, vb_s, kbe_s, qe_s, kh_s, bx_s, gcx_s, egl_s, o_s, *, tb):
    @pl.when(pl.program_id(0) == 0)
    def _():
        carry_ref[...] = jnp.zeros_like(carry_ref)
        st_ref[...] = jnp.zeros_like(st_ref)

    xin = p_ref[:, 0:GDN_CONV_CH]
    cw = cw_ref[...]
    carry = carry_ref[...]
    acc = xin * cw[GDN_CONV - 1:GDN_CONV]
    for j in range(1, GDN_CONV):
        acc = acc + _shift_rows(carry, xin, j) * cw[GDN_CONV - 1 - j:GDN_CONV - j]
    carry_ref[...] = xin[tb - SUBLANES:]
    qkv = acc * _sigmoid(acc)

    def l2n(z):
        return z * lax.rsqrt(jnp.sum(z * z, axis=-1, keepdims=True) + 1e-6)

    hd = GDN_HEAD_DIM
    for h in range(GDN_QK_HEADS):
        q_s[:, h * hd:(h + 1) * hd] = l2n(qkv[:, h * hd:(h + 1) * hd]) * (hd ** -0.5)
        k_s[:, h * hd:(h + 1) * hd] = l2n(qkv[:, GDN_QK_WIDTH + h * hd:GDN_QK_WIDTH + (h + 1) * hd])
    v = qkv[:, 2 * GDN_QK_WIDTH:]

    ba = p_ref[:, GDN_OFF_BA:GDN_PAD_COLS]
    beta = _sigmoid(ba)
    gstep = -jnp.exp(alog_ref[...]) * _softplus(ba + dtb_ref[...])
    tri, tot = _chunk_masks(tb)
    gc = _mm(tri, gstep, "nn", 1, 3)
    gl = _mm(tot, gstep, "nn", 1, 3)
    hh = _iota((LANES, GDN_V_WIDTH), 0)
    blk = lax.shift_right_logical(_iota((LANES, GDN_V_WIDTH), 1), hd.bit_length() - 1)
    e_beta = _ones_where(hh == blk)
    e_g = _ones_where(hh == blk + GDN_V_HEADS)
    bx = _mm(beta, e_beta, "nn", 3, 1)
    gcx = _mm(gc, e_g, "nn", 3, 1)
    glx = _mm(gl, e_g, "nn", 3, 1)
    bx_s[...] = bx
    gcx_s[...] = gcx
    egl_s[...] = jnp.exp(glx)
    egc = jnp.exp(gcx)
    ehat = jnp.exp(glx - gcx)
    vb_s[...] = v * bx
    rep = GDN_V_HEADS // GDN_QK_HEADS
    for h in range(GDN_V_HEADS):
        sl = slice(h * hd, (h + 1) * hd)
        qs = slice((h // rep) * hd, (h // rep + 1) * hd)
        kh = k_s[:, qs]
        kbe_s[:, sl] = kh * bx[:, sl] * egc[:, sl]
        qe_s[:, sl] = q_s[:, qs] * egc[:, sl]
        kh_s[:, sl] = kh * ehat[:, sl]

    L = CHUNK
    P2 = 2 * L
    pr = GDN_PREC
    rr = _iota((P2, P2), 0)
    cc = _iota((P2, P2), 1)
    same_blk = (rr >= L) == (cc >= L)
    lr = rr & (L - 1)
    lc = cc & (L - 1)
    strict_bd = same_blk & (lr > lc)
    incl_bd = same_blk & (lr >= lc)

    def chunk(ci, carry_):
        r0 = pl.multiple_of(ci * L, L)
        rows = pl.ds(r0, L)
        for p in range(GDN_QK_HEADS):
            qs = slice(p * hd, (p + 1) * hd)
            h0 = slice((2 * p) * hd, (2 * p + 1) * hd)
            h1 = slice((2 * p + 1) * hd, (2 * p + 2) * hd)
            kp = k_s[rows, qs]
            qp = q_s[rows, qs]
            kk2 = jnp.concatenate([kp, kp], 0)
            gm = _mm(jnp.concatenate([kp, qp], 0), kk2, "nt", pr, pr)
            gcol = jnp.concatenate([gcx_s[rows, h0], gcx_s[rows, h1]], 0)
            diff = gcol - gcol.T
            decay = jnp.where(incl_bd, jnp.exp(jnp.where(incl_bd, diff, 0.0)), 0.0)
            bcol = jnp.concatenate([bx_s[rows, h0], bx_s[rows, h1]], 0)
            a_bd = jnp.where(strict_bd, -(jnp.concatenate([gm[:L], gm[:L]], 0) * bcol * decay), 0.0)
            inv = _unit_lower_inverse(a_bd, pr)
            a_qk = jnp.concatenate([gm[L:], gm[L:]], 0) * decay
            rhs = jnp.concatenate([jnp.concatenate([vb_s[rows, h0], kbe_s[rows, h0]], 1),
                                   jnp.concatenate([vb_s[rows, h1], kbe_s[rows, h1]], 1)], 0)
            uw = _mm(inv, rhs, "nn", pr, pr)
            vnew = []
            xs = []
            for hi, hs in enumerate((h0, h1)):
                s0 = st_ref[2 * p + hi]
                u_h = uw[hi * L:(hi + 1) * L, :hd]
                w_h = uw[hi * L:(hi + 1) * L, hd:]
                x_h = _mm(jnp.concatenate([w_h, qe_s[rows, hs]], 0), s0, "nn", pr, pr)
                vnew.append(u_h - x_h[:L])
                xs.append(x_h[L:])
            o_pair = _mm(a_qk, jnp.concatenate(vnew, 0), "nn", pr, pr)
            for hi, hs in enumerate((h0, h1)):
                o_s[rows, hs] = xs[hi] + o_pair[hi * L:(hi + 1) * L]
                s0 = st_ref[2 * p + hi]
                upd = _mm(kh_s[rows, hs], vnew[hi], "tn", pr, pr)
                st_ref[2 * p + hi] = s0 * egl_s[pl.ds(r0, 1), hs] + upd
        return carry_

    lax.fori_loop(0, tb // L, chunk, 0)

    z = p_ref[:, GDN_OFF_Z:GDN_OFF_BA]
    nw = nw_ref[...]
    for h in range(GDN_V_HEADS):
        sl = slice(h * hd, (h + 1) * hd)
        o = o_s[:, sl]
        zz = z[:, sl]
        on = o * lax.rsqrt(jnp.mean(o * o, axis=-1, keepdims=True) + RMS_EPS) * nw
        o_ref[:, sl] = (on * (zz * _sigmoid(zz))).astype(o_ref.dtype)


def _gdn(p_gdn, conv_w, alog_pad, dtb_pad, norm_w, tb=256):
    t = p_gdn.shape[0]
    tb = min(tb, t)

    def full(a):
        return pl.BlockSpec(a.shape, lambda i: (0,) * a.ndim)

    params = (conv_w, alog_pad, dtb_pad, norm_w)
    qk = pltpu.VMEM((tb, GDN_QK_WIDTH), F32)
    vw = pltpu.VMEM((tb, GDN_V_WIDTH), F32)
    return pl.pallas_call(
        functools.partial(_gdn_kernel, tb=tb),
        grid=(t // tb,),
        in_specs=[pl.BlockSpec((tb, GDN_PAD_COLS), lambda i: (i, 0))] + [full(a) for a in params],
        out_specs=pl.BlockSpec((tb, GDN_V_WIDTH), lambda i: (i, 0)),
        out_shape=jax.ShapeDtypeStruct((t, GDN_V_WIDTH), BF16),
        scratch_shapes=[pltpu.VMEM((SUBLANES, GDN_CONV_CH), F32),
                        pltpu.VMEM((GDN_V_HEADS, GDN_HEAD_DIM, GDN_HEAD_DIM), F32),
                        qk, qk] + [vw] * 8,
        compiler_params=pltpu.CompilerParams(
            dimension_semantics=("arbitrary",), vmem_limit_bytes=VMEM_LIMIT),
        name="gdn",
    )(p_gdn, *params)


def _merge_kernel(u_ref, ya_ref, yb_ref, wga_ref, wgb_ref, la_ref, lb_ref, o_ref):
    u = u_ref[...]
    ga = _sigmoid(jnp.dot(u, wga_ref[...], preferred_element_type=F32))
    gb = _sigmoid(jnp.dot(u, wgb_ref[...], preferred_element_type=F32))
    ma = jnp.dot(ya_ref[...], la_ref[...], preferred_element_type=F32)
    mb = jnp.dot(yb_ref[...], lb_ref[...], preferred_element_type=F32)
    o_ref[...] = (ga * ma + gb * mb).astype(o_ref.dtype)


def _merge(u, ya, yb, wga, wgb, la, lb, tm=1024, tn=512):
    t, d = u.shape
    tm = min(tm, t)
    bw = ya.shape[1]
    return pl.pallas_call(
        _merge_kernel,
        grid=(d // tn, t // tm),
        in_specs=[
            pl.BlockSpec((tm, d), lambda c, i: (i, 0)),
            pl.BlockSpec((tm, bw), lambda c, i: (i, 0)),
            pl.BlockSpec((tm, bw), lambda c, i: (i, 0)),
            pl.BlockSpec((d, tn), lambda c, i: (0, c)),
            pl.BlockSpec((d, tn), lambda c, i: (0, c)),
            pl.BlockSpec((bw, tn), lambda c, i: (0, c)),
            pl.BlockSpec((bw, tn), lambda c, i: (0, c)),
        ],
        out_specs=pl.BlockSpec((tm, tn), lambda c, i: (i, c)),
        out_shape=jax.ShapeDtypeStruct((t, d), BF16),
        compiler_params=pltpu.CompilerParams(
            dimension_semantics=("arbitrary", "arbitrary"), vmem_limit_bytes=VMEM_LIMIT),
        name="merge",
    )(u, ya, yb, wga, wgb, la, lb)


def _resid_kernel(x_ref, w_ref, r_ref, g_ref, h_ref, *rest, emit):
    h = r_ref[...] + jnp.dot(x_ref[...], w_ref[...], preferred_element_type=F32)
    h_ref[...] = h
    if emit:
        rest[0][...] = _rms(h, g_ref[...]).astype(BF16)


def _resid_proj(x, w, resid, g, emit, name, tm=512):
    t, k = x.shape
    d = w.shape[1]
    tm = min(tm, t)
    out_shape = [jax.ShapeDtypeStruct((t, d), F32)]
    out_specs = [pl.BlockSpec((tm, d), lambda i: (i, 0))]
    if emit:
        out_shape.append(jax.ShapeDtypeStruct((t, d), BF16))
        out_specs.append(pl.BlockSpec((tm, d), lambda i: (i, 0)))
    return pl.pallas_call(
        functools.partial(_resid_kernel, emit=emit),
        grid=(t // tm,),
        in_specs=[
            pl.BlockSpec((tm, k), lambda i: (i, 0)),
            pl.BlockSpec((k, d), lambda i: (0, 0)),
            pl.BlockSpec((tm, d), lambda i: (i, 0)),
            pl.BlockSpec((1, d), lambda i: (0, 0)),
        ],
        out_specs=out_specs,
        out_shape=out_shape,
        compiler_params=pltpu.CompilerParams(
            dimension_semantics=("arbitrary",), vmem_limit_bytes=VMEM_LIMIT),
        name=name,
    )(x, w, resid, g)


def _memkv_kernel(m_ref, g_ref, wk_ref, wv_ref, k_ref, v_ref):
    mn = _rms(m_ref[...], g_ref[...]).astype(BF16)
    k_ref[...] = jnp.dot(mn, wk_ref[...], preferred_element_type=F32).astype(BF16)
    v_ref[...] = jnp.dot(mn, wv_ref[...], preferred_element_type=F32).astype(BF16)


def _memkv(mem, g, wk, wv, tn=512):
    m, d = mem.shape
    return pl.pallas_call(
        _memkv_kernel,
        grid=(d // tn,),
        in_specs=[
            pl.BlockSpec((m, d), lambda c: (0, 0)),
            pl.BlockSpec((1, d), lambda c: (0, 0)),
            pl.BlockSpec((d, tn), lambda c: (0, c)),
            pl.BlockSpec((d, tn), lambda c: (0, c)),
        ],
        out_specs=[pl.BlockSpec((m, tn), lambda c: (0, c))] * 2,
        out_shape=[jax.ShapeDtypeStruct((m, d), BF16)] * 2,
        compiler_params=pltpu.CompilerParams(
            dimension_semantics=("arbitrary",), vmem_limit_bytes=VMEM_LIMIT),
        name="memkv",
    )(mem, g, wk, wv)


def _xattn_kernel(hq_ref, wq_ref, k_ref, v_ref, o_ref):
    q = jnp.dot(hq_ref[...], wq_ref[...], preferred_element_type=F32)
    hd = MEM_HEAD_DIM
    for h in range(MEM_HEADS):
        sl = slice(h * hd, (h + 1) * hd)
        s = _mm(q[:, sl].astype(BF16), k_ref[:, sl], "nt") * (hd ** -0.5)
        e = jnp.exp(s - jnp.max(s, axis=-1, keepdims=True))
        p = e / jnp.sum(e, axis=-1, keepdims=True)
        o_ref[:, sl] = jnp.dot(p.astype(BF16), v_ref[:, sl], preferred_element_type=F32).astype(BF16)


def _xattn(hq, wq, k, v, tm=512):
    t, d = hq.shape
    tm = min(tm, t)
    m = k.shape[0]
    return pl.pallas_call(
        _xattn_kernel,
        grid=(t // tm,),
        in_specs=[
            pl.BlockSpec((tm, d), lambda i: (i, 0)),
            pl.BlockSpec((d, d), lambda i: (0, 0)),
            pl.BlockSpec((m, d), lambda i: (0, 0)),
            pl.BlockSpec((m, d), lambda i: (0, 0)),
        ],
        out_specs=pl.BlockSpec((tm, d), lambda i: (i, 0)),
        out_shape=jax.ShapeDtypeStruct((t, d), BF16),
        compiler_params=pltpu.CompilerParams(
            dimension_semantics=("arbitrary",), vmem_limit_bytes=VMEM_LIMIT),
        name="xattn",
    )(hq, wq, k, v)


def _pad_cols(a, n):
    return jnp.pad(a, ((0, 0), (0, n - a.shape[1])))


def _pad_rows(a, n):
    return jnp.pad(a, ((0, n - a.shape[0]), (0, 0)))


def _rw_layout(a):
    o1 = 3 * RW_WIDTH
    o2 = o1 + RW_DECAY_LORA
    o3 = o2 + RW_AAA_LORA
    return jnp.concatenate([
        a[:, :o1],
        _pad_cols(a[:, o1:o2], LANES),
        _pad_cols(a[:, o2:o3], LANES),
        _pad_cols(a[:, o3:], 2 * LANES),
    ], axis=1)


def kernel(x, mem, ffn1_norm, ffn1_w_gate, ffn1_w_up, ffn1_w_down, mix_norm, w_in, rw_shift_mu, rw_w0, rw_w2, rw_a0, rw_a2, rw_g2, rw_k_k, rw_k_a, rw_r_k, rw_ln_w, rw_ln_b, gdn_conv_w, gdn_a_log, gdn_dt_bias, gdn_norm_w, w_lift, w_out, mem_q_norm, mem_kv_norm, w_mq, w_mk, w_mv, w_mo, ffn2_norm, ffn2_w_gate, ffn2_w_up, ffn2_w_down, final_norm):
    b_, t, d = x.shape
    assert b_ == 1 and d == D_MODEL and ffn1_norm.shape[0] == 1
    l = 0
    bf = lambda a: a.astype(BF16)
    row = lambda a: a.reshape(1, -1)

    h0 = x.reshape(t, d)
    h1, u = _ffn(h0, row(ffn1_norm[l]), bf(ffn1_w_gate[l]), bf(ffn1_w_up[l]), bf(ffn1_w_down[l]),
                 row(mix_norm[l]), "emit")

    w_in_l = w_in[l]
    w_rw = bf(_rw_layout(w_in_l[:, :RW_COLS]))
    w_gdn = bf(_pad_cols(w_in_l[:, RW_COLS:RW_COLS + GDN_COLS], GDN_PAD_COLS))
    w_gate = w_in_l[:, RW_COLS + GDN_COLS:]
    wga = bf(w_gate[:, :D_MODEL])
    wgb = bf(w_gate[:, D_MODEL:])

    p_rw = _proj(u, w_rw, 512, "proj_rw")
    p_gdn = _proj(u, w_gdn, 640, "proj_gdn")

    y_rw = _rwkv(
        p_rw, _rw_layout(row(rw_shift_mu[l])), row(rw_w0[l]), _pad_rows(rw_w2[l], LANES),
        row(rw_a0[l]), _pad_rows(rw_a2[l], LANES), _pad_rows(rw_g2[l], 2 * LANES),
        row(rw_k_k[l]), row(rw_k_a[l]), row(rw_r_k[l]), row(rw_ln_w[l]), row(rw_ln_b[l]))

    nh = GDN_V_HEADS
    alog_pad = jnp.pad(row(gdn_a_log[l]), ((0, 0), (nh, LANES - 2 * nh)))
    dtb_pad = jnp.pad(row(gdn_dt_bias[l]), ((0, 0), (nh, LANES - 2 * nh)))
    y_gdn = _gdn(p_gdn, gdn_conv_w[l], alog_pad, dtb_pad, row(gdn_norm_w[l]))

    merged = _merge(u, y_rw, y_gdn, wga, wgb, bf(w_lift[l, 0]), bf(w_lift[l, 1]))
    h2, hq = _resid_proj(merged, bf(w_out[l]), h1, row(mem_q_norm[l]), True, "out_proj")

    mk, mv = _memkv(mem.reshape(mem.shape[1], d), row(mem_kv_norm[l]), bf(w_mk[l]), bf(w_mv[l]))
    att = _xattn(hq, bf(w_mq[l]), mk, mv)
    (h3,) = _resid_proj(att, bf(w_mo[l]), h2, row(mem_q_norm[l]), False, "attn_out")

    (y,) = _ffn(h3, row(ffn2_norm[l]), bf(ffn2_w_gate[l]), bf(ffn2_w_up[l]), bf(ffn2_w_down[l]),
                row(final_norm), "final")
    return y.reshape(b_, t, d)
```

```python
import functools

import jax
import jax.numpy as jnp
from jax import lax
from jax.experimental import pallas as pl
from jax.experimental.pallas import tpu as pltpu

F32 = jnp.float32
BF16 = jnp.bfloat16

D_MODEL = 2048
D_FF = 5632
RMS_EPS = 1e-6

RW_HEADS = 16
RW_HEAD_DIM = 64
RW_WIDTH = 1024
RW_DECAY_LORA = 64
RW_AAA_LORA = 64
RW_GATE_LORA = 160
RW_GN_EPS = 64e-5
RW_COLS = 3 * RW_WIDTH + RW_DECAY_LORA + RW_AAA_LORA + RW_GATE_LORA
LANES = 128
RW_OFF_WL = 3 * RW_WIDTH
RW_OFF_AL = RW_OFF_WL + LANES
RW_OFF_GL = RW_OFF_AL + LANES
RW_PAD_COLS = RW_OFF_GL + 2 * LANES

GDN_QK_HEADS = 4
GDN_V_HEADS = 8
GDN_HEAD_DIM = 128
GDN_QK_WIDTH = 512
GDN_V_WIDTH = 1024
GDN_CONV = 4
GDN_CONV_CH = 2 * GDN_QK_WIDTH + GDN_V_WIDTH
GDN_COLS = GDN_CONV_CH + GDN_V_WIDTH + 2 * GDN_V_HEADS
GDN_OFF_Z = GDN_CONV_CH
GDN_OFF_BA = GDN_CONV_CH + GDN_V_WIDTH
GDN_PAD_COLS = GDN_OFF_BA + LANES

MEM_HEADS = 4
MEM_HEAD_DIM = D_MODEL // MEM_HEADS

CHUNK = 64
SUBLANES = 8
VMEM_LIMIT = 56 * 1024 * 1024

_DIMS = {
    "nn": (((1,), (0,)), ((), ())),
    "nt": (((1,), (1,)), ((), ())),
    "tn": (((0,), (0,)), ((), ())),
}


def _split_bf16(x, n):
    if x.dtype == BF16:
        return [x]
    parts = []
    r = x
    for i in range(n):
        p = r.astype(BF16)
        parts.append(p)
        if i + 1 < n:
            r = r - p.astype(F32)
    return parts


def _mm(a, b, dims="nn", pa=1, pb=1):
    ap = _split_bf16(a, pa)
    bp = _split_bf16(b, pb)
    n = max(len(ap), len(bp))
    acc = None
    for i, ai in enumerate(ap):
        for j, bj in enumerate(bp):
            if i + j < n:
                t = lax.dot_general(ai, bj, _DIMS[dims], preferred_element_type=F32)
                acc = t if acc is None else acc + t
    return acc


def _rms(x, g, eps=RMS_EPS):
    ms = jnp.mean(x * x, axis=-1, keepdims=True)
    return x * lax.rsqrt(ms + eps) * g


def _sigmoid(x):
    return 1.0 / (1.0 + jnp.exp(-x))


def _softplus(x):
    return jnp.maximum(x, 0.0) + jnp.log(1.0 + jnp.exp(-jnp.abs(x)))


def _iota(shape, axis):
    return lax.broadcasted_iota(jnp.int32, shape, axis)


def _ones_where(cond):
    return jnp.where(cond, 1.0, 0.0).astype(BF16)


def _dot(a, b, dims="nn"):
    return lax.dot_general(a.astype(BF16), b.astype(BF16), _DIMS[dims], preferred_element_type=F32)


def _group_sum(x, group):
    shift = group.bit_length() - 1
    gi = lax.shift_right_logical(_iota((LANES, LANES), 0), shift)
    gj = lax.shift_right_logical(_iota((LANES, LANES), 1), shift)
    bd = _ones_where(gi == gj)
    xb = x.astype(BF16)
    outs = [jnp.dot(xb[:, c * LANES:(c + 1) * LANES], bd, preferred_element_type=F32)
            for c in range(x.shape[-1] // LANES)]
    return outs[0] if len(outs) == 1 else jnp.concatenate(outs, axis=-1)


def _chunk_prefix_mask(tb):
    shift = CHUNK.bit_length() - 1
    i = _iota((tb, tb), 0)
    j = _iota((tb, tb), 1)
    same = lax.shift_right_logical(i, shift) == lax.shift_right_logical(j, shift)
    return _ones_where(same & (i >= j))


def _shift_rows(carry, x, j):
    xx = jnp.concatenate([carry, x], axis=0)
    return pltpu.roll(xx, j, 0)[SUBLANES:]


def _unit_lower_inverses(mats):
    n = mats[0].shape[0]
    eye = jnp.where(_iota((n, n), 0) == _iota((n, n), 1), 1.0, 0.0)
    invs = [eye + a for a in mats]
    pws = [a.astype(BF16) for a in mats]
    for _ in range(CHUNK.bit_length() - 2):
        pws = [_dot(pw, pw).astype(BF16) for pw in pws]
        invs = [inv + _dot(inv, pw) for inv, pw in zip(invs, pws)]
    return invs


def _ffn_kernel(x_ref, g_ref, wg_ref, wu_ref, wd_ref, g2_ref, o_ref, *rest, nj, mode):
    if mode == "emit":
        u_ref, xn_ref, acc_ref = rest
    else:
        xn_ref, acc_ref = rest
    j = pl.program_id(1)

    @pl.when(j == 0)
    def _():
        xn_ref[...] = _rms(x_ref[...], g_ref[...]).astype(BF16)
        acc_ref[...] = jnp.zeros_like(acc_ref)

    xn = xn_ref[...]
    gate = jnp.dot(xn, wg_ref[...], preferred_element_type=F32)
    up = jnp.dot(xn, wu_ref[...], preferred_element_type=F32)
    mid = (gate * _sigmoid(gate) * up).astype(BF16)
    acc_ref[...] += jnp.dot(mid, wd_ref[...], preferred_element_type=F32)

    @pl.when(j == nj - 1)
    def _():
        h = x_ref[...] + 0.5 * acc_ref[...]
        if mode == "emit":
            o_ref[...] = h
            u_ref[...] = _rms(h, g2_ref[...]).astype(BF16)
        else:
            o_ref[...] = _rms(h, g2_ref[...])


def _ffn(x, g, wg, wu, wd, g2, mode, tm=512, tf=512):
    t, d = x.shape
    f = wg.shape[1]
    nj = f // tf
    out_shape = [jax.ShapeDtypeStruct((t, d), F32)]
    out_specs = [pl.BlockSpec((tm, d), lambda i, j: (i, 0))]
    if mode == "emit":
        out_shape.append(jax.ShapeDtypeStruct((t, d), BF16))
        out_specs.append(pl.BlockSpec((tm, d), lambda i, j: (i, 0)))
    res = pl.pallas_call(
        functools.partial(_ffn_kernel, nj=nj, mode=mode),
        grid=(t // tm, nj),
        in_specs=[
            pl.BlockSpec((tm, d), lambda i, j: (i, 0)),
            pl.BlockSpec((1, d), lambda i, j: (0, 0)),
            pl.BlockSpec((d, tf), lambda i, j: (0, j)),
            pl.BlockSpec((d, tf), lambda i, j: (0, j)),
            pl.BlockSpec((tf, d), lambda i, j: (j, 0)),
            pl.BlockSpec((1, d), lambda i, j: (0, 0)),
        ],
        out_specs=out_specs,
        out_shape=out_shape,
        scratch_shapes=[pltpu.VMEM((tm, d), BF16), pltpu.VMEM((tm, d), F32)],
        compiler_params=pltpu.CompilerParams(
            dimension_semantics=("arbitrary", "arbitrary"), vmem_limit_bytes=VMEM_LIMIT),
        name="ffn_" + mode,
    )(x, g, wg, wu, wd, g2)
    return res


def _proj_kernel(x_ref, w_ref, o_ref):
    o_ref[...] = jnp.dot(x_ref[...], w_ref[...], preferred_element_type=F32).astype(o_ref.dtype)


def _proj(x, w, tn, name, tm=1024, out_dtype=F32):
    t, k = x.shape
    n = w.shape[1]
    tm = min(tm, t)
    return pl.pallas_call(
        _proj_kernel,
        grid=(n // tn, t // tm),
        in_specs=[
            pl.BlockSpec((tm, k), lambda c, i: (i, 0)),
            pl.BlockSpec((k, tn), lambda c, i: (0, c)),
        ],
        out_specs=pl.BlockSpec((tm, tn), lambda c, i: (i, c)),
        out_shape=jax.ShapeDtypeStruct((t, n), out_dtype),
        compiler_params=pltpu.CompilerParams(
            dimension_semantics=("arbitrary", "arbitrary"), vmem_limit_bytes=VMEM_LIMIT),
        name=name,
    )(x, w)


RW_LORA_PREC = 2


def _rwkv_kernel(p_ref, mu_ref, w0_ref, w2_ref, a0_ref, a2_ref, g2_ref, kk_ref, ka_ref, rk_ref,
                 lnw_ref, lnb_ref, o_ref,
                 carry_ref, st_ref, at_s, rt_s, bt_s, kt_s, v_s, c_s, y_s, bonus_s, g_s,
                 *, tb):
    @pl.when(pl.program_id(0) == 0)
    def _():
        carry_ref[...] = jnp.zeros_like(carry_ref)
        st_ref[...] = jnp.zeros_like(st_ref)

    cols = p_ref[...]
    prev = _shift_rows(carry_ref[...], cols, 1)
    carry_ref[...] = cols[tb - SUBLANES:]
    x = cols + (prev - cols) * mu_ref[...]

    r = x[:, 0:RW_WIDTH]
    k = x[:, RW_WIDTH:2 * RW_WIDTH]
    v = x[:, 2 * RW_WIDTH:3 * RW_WIDTH]
    wl = x[:, RW_OFF_WL:RW_OFF_AL]
    al = x[:, RW_OFF_AL:RW_OFF_GL]
    gl = x[:, RW_OFF_GL:RW_PAD_COLS]

    lp = RW_LORA_PREC
    zw = w0_ref[...] + _mm(jnp.tanh(wl), w2_ref[...], "nn", lp, lp)
    w_log = -_softplus(-zw) - 0.5
    lw = -jnp.exp(w_log)
    a = _sigmoid(a0_ref[...] + _mm(al, a2_ref[...], "nn", lp, lp))
    g_s[...] = _mm(_sigmoid(gl), g2_ref[...], "nn", lp, lp)

    kkv = k * kk_ref[...]
    kkn = kkv * lax.rsqrt(_group_sum(kkv * kkv, RW_HEAD_DIM) + 1e-6)
    k2 = k * (1.0 + (a - 1.0) * ka_ref[...])
    aa = -kkn
    bb = kkn * a
    bonus_s[...] = _group_sum(r * k2 * rk_ref[...], RW_HEAD_DIM) * v

    c = _mm(_chunk_prefix_mask(tb), lw, "nn", 1, 2)
    einv = jnp.exp(-c)
    rt_s[...] = (r * jnp.exp(c)).astype(BF16)
    at_s[...] = (aa * jnp.exp(c - lw)).astype(BF16)
    bt_s[...] = (bb * einv).astype(BF16)
    kt_s[...] = (k2 * einv).astype(BF16)
    c_s[...] = c
    v_s[...] = v.astype(BF16)

    L = CHUNK
    P2 = 2 * L
    n_pairs = RW_HEADS // 2
    lane_l = _iota((L, P2), 1)
    row_l = _iota((L, P2), 0)
    h0_l = lane_l < L
    strict_l = row_l > (lane_l & (L - 1))
    incl_l = row_l >= (lane_l & (L - 1))
    rr = _iota((P2, P2), 0)
    cc = _iota((P2, P2), 1)
    same_blk = ((rr ^ cc) & L) == 0
    strict_bd = same_blk & ((rr & (L - 1)) > (cc & (L - 1)))
    top = rr < L

    def only0(z):
        return jnp.where(h0_l, z, jnp.zeros_like(z))

    def only1(z):
        return jnp.where(h0_l, jnp.zeros_like(z), z)

    def chunk(ci, carry):
        r0 = pl.multiple_of(ci * L, L)
        rows = pl.ds(r0, L)
        sls = [slice(p * P2, (p + 1) * P2) for p in range(n_pairs)]
        at = [at_s[rows, s] for s in sls]
        rt = [rt_s[rows, s] for s in sls]
        bt = [bt_s[rows, s] for s in sls]
        kt = [kt_s[rows, s] for s in sls]
        vv = [v_s[rows, s] for s in sls]
        s0 = [st_ref[p] for p in range(n_pairs)]
        g0 = [_dot(jnp.concatenate([only0(at[p]), only0(rt[p])], 0),
                   jnp.concatenate([bt[p], kt[p]], 0), "nt") for p in range(n_pairs)]
        g1 = [_dot(jnp.concatenate([only1(rt[p]), only1(at[p])], 0),
                   jnp.concatenate([kt[p], bt[p]], 0), "nt") for p in range(n_pairs)]
        xs = [_dot(jnp.concatenate([at[p], rt[p]], 0), s0[p], "nt") for p in range(n_pairs)]
        inv = _unit_lower_inverses(
            [jnp.where(strict_bd, jnp.where(top, g0[p], g1[p]), 0.0) for p in range(n_pairs)])
        v0 = [only0(z) for z in vv]
        v1 = [only1(z) for z in vv]
        rhs = [xs[p][:L] + _dot(jnp.where(strict_l, jnp.where(h0_l, g1[p][L:], g0[p][:L]), 0.0),
                                jnp.concatenate([v1[p], v0[p]], 0)) for p in range(n_pairs)]
        pp = [_dot(inv[p], jnp.concatenate([only0(rhs[p]), only1(rhs[p])], 0)) for p in range(n_pairs)]
        pm = [(z[:L] + z[L:]).astype(BF16) for z in pp]
        for p in range(n_pairs):
            ly = jnp.concatenate([jnp.where(incl_l, g0[p][L:], 0.0), jnp.where(incl_l, g1[p][:L], 0.0)], 1)
            ry = jnp.concatenate([only0(pm[p]), v0[p], v1[p], only1(pm[p])], 0)
            y_s[rows, sls[p]] = xs[p][L:] + _dot(ly, ry)
        for p in range(n_pairs):
            upd = _dot(jnp.concatenate([pm[p], vv[p]], 0), jnp.concatenate([bt[p], kt[p]], 0), "tn")
            tail = c_s[pl.ds(pl.multiple_of(r0 + (L - SUBLANES), SUBLANES), SUBLANES), sls[p]]
            w_end = jnp.exp(tail[SUBLANES - 1:])
            st_ref[p] = (s0[p] + jnp.where(same_blk, upd, 0.0)) * w_end
        return carry

    lax.fori_loop(0, tb // L, chunk, 0)

    y = y_s[...]
    inv_n = 1.0 / RW_HEAD_DIM
    mu = _group_sum(y, RW_HEAD_DIM) * inv_n
    yc = y - mu
    var = _group_sum(yc * yc, RW_HEAD_DIM) * inv_n
    yn = yc * lax.rsqrt(var + RW_GN_EPS) * lnw_ref[...] + lnb_ref[...]
    o_ref[...] = ((yn + bonus_s[...]) * g_s[...]).astype(o_ref.dtype)


def _rwkv(p_rw, mu, w0, w2, a0, a2, g2, k_k, k_a, r_k, ln_w, ln_b, tb=256):
    t = p_rw.shape[0]
    tb = min(tb, t)
    w = RW_WIDTH

    def full(a):
        return pl.BlockSpec(a.shape, lambda i: (0,) * a.ndim)

    params = (mu, w0, w2, a0, a2, g2, k_k, k_a, r_k, ln_w, ln_b)
    return pl.pallas_call(
        functools.partial(_rwkv_kernel, tb=tb),
        grid=(t // tb,),
        in_specs=[pl.BlockSpec((tb, RW_PAD_COLS), lambda i: (i, 0))] + [full(a) for a in params],
        out_specs=pl.BlockSpec((tb, w), lambda i: (i, 0)),
        out_shape=jax.ShapeDtypeStruct((t, w), BF16),
        scratch_shapes=[pltpu.VMEM((SUBLANES, RW_PAD_COLS), F32),
                        pltpu.VMEM((RW_HEADS // 2, 2 * CHUNK, 2 * CHUNK), F32)]
        + [pltpu.VMEM((tb, w), BF16)] * 5 + [pltpu.VMEM((tb, w), F32)] * 4,
        compiler_params=pltpu.CompilerParams(
            dimension_semantics=("arbitrary",), vmem_limit_bytes=VMEM_LIMIT),
        name="rwkv7",
    )(p_rw, *params)


def _gdn_kernel(p_ref, cw_ref, alog_ref, dtb_ref, nw_ref, o_ref,
                carry_ref, st_ref, q_s, k_s, vb_s, kbe_s, qe_s, kh_s, bx_s, gcx_s, o_s, *, tb):
    @pl.when(pl.program_id(0) == 0)
    def _():
        carry_ref[...] = jnp.zeros_like(carry_ref)
        st_ref[...] = jnp.zeros_like(st_ref)

    xin = p_ref[:, 0:GDN_CONV_CH]
    cw = cw_ref[...]
    carry = carry_ref[...]
    acc = xin * cw[GDN_CONV - 1:GDN_CONV]
    for j in range(1, GDN_CONV):
        acc = acc + _shift_rows(carry, xin, j) * cw[GDN_CONV - 1 - j:GDN_CONV - j]
    carry_ref[...] = xin[tb - SUBLANES:]
    qkv = acc * _sigmoid(acc)

    def l2n(z):
        return z * lax.rsqrt(jnp.sum(z * z, axis=-1, keepdims=True) + 1e-6)

    hd = GDN_HEAD_DIM
    qn = []
    kn = []
    for h in range(GDN_QK_HEADS):
        sl = slice(h * hd, (h + 1) * hd)
        qn.append(l2n(qkv[:, sl]) * (hd ** -0.5))
        kn.append(l2n(qkv[:, GDN_QK_WIDTH + h * hd:GDN_QK_WIDTH + (h + 1) * hd]))
        q_s[:, sl] = qn[h].astype(BF16)
        k_s[:, sl] = kn[h].astype(BF16)
    v = qkv[:, 2 * GDN_QK_WIDTH:]

    ba = p_ref[:, GDN_OFF_BA:GDN_PAD_COLS]
    beta = _sigmoid(ba)
    gstep = -jnp.exp(alog_ref[...]) * _softplus(ba + dtb_ref[...])
    gc = _mm(_chunk_prefix_mask(tb), gstep, "nn", 1, 3)
    hh = _iota((LANES, GDN_V_WIDTH), 0)
    blk = lax.shift_right_logical(_iota((LANES, GDN_V_WIDTH), 1), hd.bit_length() - 1)
    e_beta = _ones_where(hh == blk)
    e_g = _ones_where(hh == blk + GDN_V_HEADS)
    bx = _mm(beta, e_beta, "nn", 2, 1)
    gcx = _mm(gc, e_g, "nn", 3, 1)
    bx_s[...] = bx
    gcx_s[...] = gcx
    egc = jnp.exp(gcx)
    vb_s[...] = (v * bx).astype(BF16)
    L = CHUNK
    rep = GDN_V_HEADS // GDN_QK_HEADS
    for h in range(GDN_V_HEADS):
        sl = slice(h * hd, (h + 1) * hd)
        kbe_s[:, sl] = (kn[h // rep] * bx[:, sl] * egc[:, sl]).astype(BF16)
        qe_s[:, sl] = (qn[h // rep] * egc[:, sl]).astype(BF16)
        for ci in range(tb // L):
            rs = slice(ci * L, (ci + 1) * L)
            to_end = jnp.exp(gcx[(ci + 1) * L - 1:(ci + 1) * L, sl] - gcx[rs, sl])
            kh_s[rs, sl] = (kn[h // rep][rs] * to_end).astype(BF16)

    P2 = 2 * L
    rr = _iota((P2, P2), 0)
    cc = _iota((P2, P2), 1)
    same_blk = ((rr ^ cc) & L) == 0
    lr = rr & (L - 1)
    lc = cc & (L - 1)
    strict_bd = same_blk & (lr > lc)
    incl_bd = same_blk & (lr >= lc)
    n_pairs = GDN_QK_HEADS
    hsl = [slice(h * hd, (h + 1) * hd) for h in range(GDN_V_HEADS)]

    def two_chunks(cj, carry_):
        starts = [pl.multiple_of((2 * cj + dc) * L, L) for dc in range(2)]
        items = [(r0, p) for r0 in starts for p in range(n_pairs)]
        gm = []
        for r0, p in items:
            rows = pl.ds(r0, L)
            kp = k_s[rows, hsl[p]]
            gm.append(_dot(jnp.concatenate([kp, q_s[rows, hsl[p]]], 0),
                           jnp.concatenate([kp, kp], 0), "nt"))
        a_bd = []
        a_qk = []
        for (r0, p), g in zip(items, gm):
            rows = pl.ds(r0, L)
            h0, h1 = hsl[2 * p], hsl[2 * p + 1]
            gcol = jnp.concatenate([gcx_s[rows, h0], gcx_s[rows, h1]], 0)
            diff = gcol - gcol.T
            decay = jnp.where(incl_bd, jnp.exp(jnp.where(incl_bd, diff, 0.0)), 0.0)
            bcol = jnp.concatenate([bx_s[rows, h0], bx_s[rows, h1]], 0)
            a_bd.append(jnp.where(strict_bd, -(jnp.concatenate([g[:L], g[:L]], 0) * bcol * decay), 0.0))
            a_qk.append((jnp.concatenate([g[L:], g[L:]], 0) * decay).astype(BF16))
        inv = _unit_lower_inverses(a_bd)
        uw = []
        for (r0, p), iv in zip(items, inv):
            rows = pl.ds(r0, L)
            h0, h1 = hsl[2 * p], hsl[2 * p + 1]
            rhs = jnp.concatenate([jnp.concatenate([vb_s[rows, h0], kbe_s[rows, h0]], 1),
                                   jnp.concatenate([vb_s[rows, h1], kbe_s[rows, h1]], 1)], 0)
            uw.append(_dot(iv, rhs))

        state = [st_ref[h] for h in range(GDN_V_HEADS)]
        for dc, r0 in enumerate(starts):
            rows = pl.ds(r0, L)
            x = []
            for h in range(GDN_V_HEADS):
                w_h = uw[dc * n_pairs + h // 2][(h % 2) * L:(h % 2 + 1) * L, hd:]
                x.append(_dot(jnp.concatenate([w_h.astype(BF16), qe_s[rows, hsl[h]]], 0), state[h]))
            vnew = [(uw[dc * n_pairs + h // 2][(h % 2) * L:(h % 2 + 1) * L, :hd] - x[h][:L]).astype(BF16)
                    for h in range(GDN_V_HEADS)]
            for p in range(n_pairs):
                o_pair = _dot(a_qk[dc * n_pairs + p], jnp.concatenate([vnew[2 * p], vnew[2 * p + 1]], 0))
                o_s[rows, hsl[2 * p]] = x[2 * p][L:] + o_pair[:L]
                o_s[rows, hsl[2 * p + 1]] = x[2 * p + 1][L:] + o_pair[L:]
            tail = pl.ds(pl.multiple_of(r0 + (L - SUBLANES), SUBLANES), SUBLANES)
            state = [state[h] * jnp.exp(gcx_s[tail, hsl[h]][SUBLANES - 1:])
                     + _dot(kh_s[rows, hsl[h]], vnew[h], "tn") for h in range(GDN_V_HEADS)]
        for h in range(GDN_V_HEADS):
            st_ref[h] = state[h]
        return carry_

    lax.fori_loop(0, tb // (2 * L), two_chunks, 0)

    z = p_ref[:, GDN_OFF_Z:GDN_OFF_BA]
    nw = nw_ref[...]
    for h in range(GDN_V_HEADS):
        sl = slice(h * hd, (h + 1) * hd)
        o = o_s[:, sl]
        zz = z[:, sl]
        on = o * lax.rsqrt(jnp.mean(o * o, axis=-1, keepdims=True) + RMS_EPS) * nw
        o_ref[:, sl] = (on * (zz * _sigmoid(zz))).astype(o_ref.dtype)


def _gdn(p_gdn, conv_w, alog_pad, dtb_pad, norm_w, tb=256):
    t = p_gdn.shape[0]
    tb = min(tb, t)

    def full(a):
        return pl.BlockSpec(a.shape, lambda i: (0,) * a.ndim)

    params = (conv_w, alog_pad, dtb_pad, norm_w)
    qk = pltpu.VMEM((tb, GDN_QK_WIDTH), BF16)
    return pl.pallas_call(
        functools.partial(_gdn_kernel, tb=tb),
        grid=(t // tb,),
        in_specs=[pl.BlockSpec((tb, GDN_PAD_COLS), lambda i: (i, 0))] + [full(a) for a in params],
        out_specs=pl.BlockSpec((tb, GDN_V_WIDTH), lambda i: (i, 0)),
        out_shape=jax.ShapeDtypeStruct((t, GDN_V_WIDTH), BF16),
        scratch_shapes=[pltpu.VMEM((SUBLANES, GDN_CONV_CH), F32),
                        pltpu.VMEM((GDN_V_HEADS, GDN_HEAD_DIM, GDN_HEAD_DIM), F32),
                        qk, qk]
        + [pltpu.VMEM((tb, GDN_V_WIDTH), BF16)] * 4 + [pltpu.VMEM((tb, GDN_V_WIDTH), F32)] * 3,
        compiler_params=pltpu.CompilerParams(
            dimension_semantics=("arbitrary",), vmem_limit_bytes=VMEM_LIMIT),
        name="gdn",
    )(p_gdn, *params)


def _merge_kernel(u_ref, ya_ref, yb_ref, wga_ref, wgb_ref, la_ref, lb_ref, o_ref):
    u = u_ref[...]
    ga = _sigmoid(jnp.dot(u, wga_ref[...], preferred_element_type=F32))
    gb = _sigmoid(jnp.dot(u, wgb_ref[...], preferred_element_type=F32))
    ma = jnp.dot(ya_ref[...], la_ref[...], preferred_element_type=F32)
    mb = jnp.dot(yb_ref[...], lb_ref[...], preferred_element_type=F32)
    o_ref[...] = (ga * ma + gb * mb).astype(o_ref.dtype)


def _merge(u, ya, yb, wga, wgb, la, lb, tm=1024, tn=512):
    t, d = u.shape
    tm = min(tm, t)
    bw = ya.shape[1]
    return pl.pallas_call(
        _merge_kernel,
        grid=(d // tn, t // tm),
        in_specs=[
            pl.BlockSpec((tm, d), lambda c, i: (i, 0)),
            pl.BlockSpec((tm, bw), lambda c, i: (i, 0)),
            pl.BlockSpec((tm, bw), lambda c, i: (i, 0)),
            pl.BlockSpec((d, tn), lambda c, i: (0, c)),
            pl.BlockSpec((d, tn), lambda c, i: (0, c)),
            pl.BlockSpec((bw, tn), lambda c, i: (0, c)),
            pl.BlockSpec((bw, tn), lambda c, i: (0, c)),
        ],
        out_specs=pl.BlockSpec((tm, tn), lambda c, i: (i, c)),
        out_shape=jax.ShapeDtypeStruct((t, d), BF16),
        compiler_params=pltpu.CompilerParams(
            dimension_semantics=("arbitrary", "arbitrary"), vmem_limit_bytes=VMEM_LIMIT),
        name="merge",
    )(u, ya, yb, wga, wgb, la, lb)


def _resid_kernel(x_ref, w_ref, r_ref, g_ref, h_ref, *rest, emit):
    h = r_ref[...] + jnp.dot(x_ref[...], w_ref[...], preferred_element_type=F32)
    h_ref[...] = h
    if emit:
        rest[0][...] = _rms(h, g_ref[...]).astype(BF16)


def _resid_proj(x, w, resid, g, emit, name, tm=512):
    t, k = x.shape
    d = w.shape[1]
    tm = min(tm, t)
    out_shape = [jax.ShapeDtypeStruct((t, d), F32)]
    out_specs = [pl.BlockSpec((tm, d), lambda i: (i, 0))]
    if emit:
        out_shape.append(jax.ShapeDtypeStruct((t, d), BF16))
        out_specs.append(pl.BlockSpec((tm, d), lambda i: (i, 0)))
    return pl.pallas_call(
        functools.partial(_resid_kernel, emit=emit),
        grid=(t // tm,),
        in_specs=[
            pl.BlockSpec((tm, k), lambda i: (i, 0)),
            pl.BlockSpec((k, d), lambda i: (0, 0)),
            pl.BlockSpec((tm, d), lambda i: (i, 0)),
            pl.BlockSpec((1, d), lambda i: (0, 0)),
        ],
        out_specs=out_specs,
        out_shape=out_shape,
        compiler_params=pltpu.CompilerParams(
            dimension_semantics=("arbitrary",), vmem_limit_bytes=VMEM_LIMIT),
        name=name,
    )(x, w, resid, g)


def _memkv_kernel(m_ref, g_ref, wk_ref, wv_ref, k_ref, v_ref):
    mn = _rms(m_ref[...], g_ref[...]).astype(BF16)
    k_ref[...] = jnp.dot(mn, wk_ref[...], preferred_element_type=F32).astype(BF16)
    v_ref[...] = jnp.dot(mn, wv_ref[...], preferred_element_type=F32).astype(BF16)


def _memkv(mem, g, wk, wv, tn=512):
    m, d = mem.shape
    return pl.pallas_call(
        _memkv_kernel,
        grid=(d // tn,),
        in_specs=[
            pl.BlockSpec((m, d), lambda c: (0, 0)),
            pl.BlockSpec((1, d), lambda c: (0, 0)),
            pl.BlockSpec((d, tn), lambda c: (0, c)),
            pl.BlockSpec((d, tn), lambda c: (0, c)),
        ],
        out_specs=[pl.BlockSpec((m, tn), lambda c: (0, c))] * 2,
        out_shape=[jax.ShapeDtypeStruct((m, d), BF16)] * 2,
        compiler_params=pltpu.CompilerParams(
            dimension_semantics=("arbitrary",), vmem_limit_bytes=VMEM_LIMIT),
        name="memkv",
    )(mem, g, wk, wv)


def _xattn_kernel(hq_ref, wq_ref, k_ref, v_ref, o_ref):
    q = jnp.dot(hq_ref[...], wq_ref[...], preferred_element_type=F32)
    hd = MEM_HEAD_DIM
    for h in range(MEM_HEADS):
        sl = slice(h * hd, (h + 1) * hd)
        s = _mm(q[:, sl].astype(BF16), k_ref[:, sl], "nt") * (hd ** -0.5)
        e = jnp.exp(s - jnp.max(s, axis=-1, keepdims=True))
        p = e / jnp.sum(e, axis=-1, keepdims=True)
        o_ref[:, sl] = jnp.dot(p.astype(BF16), v_ref[:, sl], preferred_element_type=F32).astype(BF16)


def _xattn(hq, wq, k, v, tm=512):
    t, d = hq.shape
    tm = min(tm, t)
    m = k.shape[0]
    return pl.pallas_call(
        _xattn_kernel,
        grid=(t // tm,),
        in_specs=[
            pl.BlockSpec((tm, d), lambda i: (i, 0)),
            pl.BlockSpec((d, d), lambda i: (0, 0)),
            pl.BlockSpec((m, d), lambda i: (0, 0)),
            pl.BlockSpec((m, d), lambda i: (0, 0)),
        ],
        out_specs=pl.BlockSpec((tm, d), lambda i: (i, 0)),
        out_shape=jax.ShapeDtypeStruct((t, d), BF16),
        compiler_params=pltpu.CompilerParams(
            dimension_semantics=("arbitrary",), vmem_limit_bytes=VMEM_LIMIT),
        name="xattn",
    )(hq, wq, k, v)


def _pad_cols(a, n):
    return jnp.pad(a, ((0, 0), (0, n - a.shape[1])))


def _pad_rows(a, n):
    return jnp.pad(a, ((0, n - a.shape[0]), (0, 0)))


def _rw_layout(a):
    o1 = 3 * RW_WIDTH
    o2 = o1 + RW_DECAY_LORA
    o3 = o2 + RW_AAA_LORA
    return jnp.concatenate([
        a[:, :o1],
        _pad_cols(a[:, o1:o2], LANES),
        _pad_cols(a[:, o2:o3], LANES),
        _pad_cols(a[:, o3:], 2 * LANES),
    ], axis=1)


def kernel(x, mem, ffn1_norm, ffn1_w_gate, ffn1_w_up, ffn1_w_down, mix_norm, w_in, rw_shift_mu, rw_w0, rw_w2, rw_a0, rw_a2, rw_g2, rw_k_k, rw_k_a, rw_r_k, rw_ln_w, rw_ln_b, gdn_conv_w, gdn_a_log, gdn_dt_bias, gdn_norm_w, w_lift, w_out, mem_q_norm, mem_kv_norm, w_mq, w_mk, w_mv, w_mo, ffn2_norm, ffn2_w_gate, ffn2_w_up, ffn2_w_down, final_norm):
    b_, t, d = x.shape
    assert b_ == 1 and d == D_MODEL and ffn1_norm.shape[0] == 1
    l = 0
    bf = lambda a: a.astype(BF16)
    row = lambda a: a.reshape(1, -1)

    h0 = x.reshape(t, d)
    h1, u = _ffn(h0, row(ffn1_norm[l]), bf(ffn1_w_gate[l]), bf(ffn1_w_up[l]), bf(ffn1_w_down[l]),
                 row(mix_norm[l]), "emit")

    w_in_l = w_in[l]
    w_rw = bf(_rw_layout(w_in_l[:, :RW_COLS]))
    w_gdn = bf(_pad_cols(w_in_l[:, RW_COLS:RW_COLS + GDN_COLS], GDN_PAD_COLS))
    w_gate = w_in_l[:, RW_COLS + GDN_COLS:]
    wga = bf(w_gate[:, :D_MODEL])
    wgb = bf(w_gate[:, D_MODEL:])

    p_rw = _proj(u, w_rw, 512, "proj_rw")
    p_gdn = _proj(u, w_gdn, 640, "proj_gdn")

    y_rw = _rwkv(
        p_rw, _rw_layout(row(rw_shift_mu[l])), row(rw_w0[l]), _pad_rows(rw_w2[l], LANES),
        row(rw_a0[l]), _pad_rows(rw_a2[l], LANES), _pad_rows(rw_g2[l], 2 * LANES),
        row(rw_k_k[l]), row(rw_k_a[l]), row(rw_r_k[l]), row(rw_ln_w[l]), row(rw_ln_b[l]))

    nh = GDN_V_HEADS
    alog_pad = jnp.pad(row(gdn_a_log[l]), ((0, 0), (nh, LANES - 2 * nh)))
    dtb_pad = jnp.pad(row(gdn_dt_bias[l]), ((0, 0), (nh, LANES - 2 * nh)))
    y_gdn = _gdn(p_gdn, gdn_conv_w[l], alog_pad, dtb_pad, row(gdn_norm_w[l]))

    merged = _merge(u, y_rw, y_gdn, wga, wgb, bf(w_lift[l, 0]), bf(w_lift[l, 1]))
    h2, hq = _resid_proj(merged, bf(w_out[l]), h1, row(mem_q_norm[l]), True, "out_proj")

    mk, mv = _memkv(mem.reshape(mem.shape[1], d), row(mem_kv_norm[l]), bf(w_mk[l]), bf(w_mv[l]))
    att = _xattn(hq, bf(w_mq[l]), mk, mv)
    (h3,) = _resid_proj(att, bf(w_mo[l]), h2, row(mem_q_norm[l]), False, "attn_out")

    (y,) = _ffn(h3, row(ffn2_norm[l]), bf(ffn2_w_gate[l]), bf(ffn2_w_up[l]), bf(ffn2_w_down[l]),
                row(final_norm), "final")
    return y.reshape(b_, t, d)
```

```python
import functools

import jax
import jax.numpy as jnp
from jax import lax
from jax.experimental import pallas as pl
from jax.experimental.pallas import tpu as pltpu

F32 = jnp.float32
BF16 = jnp.bfloat16

D_MODEL = 2048
D_FF = 5632
RMS_EPS = 1e-6

RW_HEADS = 16
RW_HEAD_DIM = 64
RW_WIDTH = 1024
RW_DECAY_LORA = 64
RW_AAA_LORA = 64
RW_GATE_LORA = 160
RW_GN_EPS = 64e-5
RW_COLS = 3 * RW_WIDTH + RW_DECAY_LORA + RW_AAA_LORA + RW_GATE_LORA
LANES = 128
RW_OFF_WL = 3 * RW_WIDTH
RW_OFF_AL = RW_OFF_WL + LANES
RW_OFF_GL = RW_OFF_AL + LANES
RW_PAD_COLS = RW_OFF_GL + 2 * LANES

GDN_QK_HEADS = 4
GDN_V_HEADS = 8
GDN_HEAD_DIM = 128
GDN_QK_WIDTH = 512
GDN_V_WIDTH = 1024
GDN_CONV = 4
GDN_CONV_CH = 2 * GDN_QK_WIDTH + GDN_V_WIDTH
GDN_COLS = GDN_CONV_CH + GDN_V_WIDTH + 2 * GDN_V_HEADS
GDN_OFF_Z = GDN_CONV_CH
GDN_OFF_BA = GDN_CONV_CH + GDN_V_WIDTH
GDN_PAD_COLS = GDN_OFF_BA + LANES

MEM_HEADS = 4
MEM_HEAD_DIM = D_MODEL // MEM_HEADS

CHUNK = 64
SUBLANES = 8
VMEM_LIMIT = 56 * 1024 * 1024

_DIMS = {
    "nn": (((1,), (0,)), ((), ())),
    "nt": (((1,), (1,)), ((), ())),
    "tn": (((0,), (0,)), ((), ())),
}


def _split_bf16(x, n):
    if x.dtype == BF16:
        return [x]
    parts = []
    r = x
    for i in range(n):
        p = r.astype(BF16)
        parts.append(p)
        if i + 1 < n:
            r = r - p.astype(F32)
    return parts


def _mm(a, b, dims="nn", pa=1, pb=1):
    ap = _split_bf16(a, pa)
    bp = _split_bf16(b, pb)
    n = max(len(ap), len(bp))
    acc = None
    for i, ai in enumerate(ap):
        for j, bj in enumerate(bp):
            if i + j < n:
                t = lax.dot_general(ai, bj, _DIMS[dims], preferred_element_type=F32)
                acc = t if acc is None else acc + t
    return acc


def _rms(x, g, eps=RMS_EPS):
    ms = jnp.mean(x * x, axis=-1, keepdims=True)
    return x * lax.rsqrt(ms + eps) * g


def _sigmoid(x):
    return 1.0 / (1.0 + jnp.exp(-x))


def _softplus(x):
    return jnp.maximum(x, 0.0) + jnp.log(1.0 + jnp.exp(-jnp.abs(x)))


def _iota(shape, axis):
    return lax.broadcasted_iota(jnp.int32, shape, axis)


def _ones_where(cond):
    return jnp.where(cond, 1.0, 0.0).astype(BF16)


def _dot(a, b, dims="nn"):
    return lax.dot_general(a.astype(BF16), b.astype(BF16), _DIMS[dims], preferred_element_type=F32)


def _group_sum(x, group):
    shift = group.bit_length() - 1
    wide = 2 * LANES
    gi = lax.shift_right_logical(_iota((wide, wide), 0), shift)
    gj = lax.shift_right_logical(_iota((wide, wide), 1), shift)
    bd = _ones_where(gi == gj)
    xb = x.astype(BF16)
    outs = [jnp.dot(xb[:, c * wide:(c + 1) * wide], bd, preferred_element_type=F32)
            for c in range(x.shape[-1] // wide)]
    return outs[0] if len(outs) == 1 else jnp.concatenate(outs, axis=-1)


def _chunk_prefix_mask(tb):
    shift = CHUNK.bit_length() - 1
    i = _iota((tb, tb), 0)
    j = _iota((tb, tb), 1)
    same = lax.shift_right_logical(i, shift) == lax.shift_right_logical(j, shift)
    return _ones_where(same & (i >= j))


def _shift_rows(carry, x, j):
    xx = jnp.concatenate([carry, x], axis=0)
    return pltpu.roll(xx, j, 0)[SUBLANES:]


def _unit_lower_inverses(mats):
    n = mats[0].shape[0]
    eye = jnp.where(_iota((n, n), 0) == _iota((n, n), 1), 1.0, 0.0)
    invs = [eye + a for a in mats]
    pws = [a.astype(BF16) for a in mats]
    for _ in range(CHUNK.bit_length() - 2):
        pws = [_dot(pw, pw).astype(BF16) for pw in pws]
        invs = [inv + _dot(inv, pw) for inv, pw in zip(invs, pws)]
    return invs


def _ffn_kernel(x_ref, g_ref, wg_ref, wu_ref, wd_ref, g2_ref, o_ref, *rest, nj, mode):
    if mode == "emit":
        u_ref, xn_ref, acc_ref = rest
    else:
        xn_ref, acc_ref = rest
    j = pl.program_id(1)

    @pl.when(j == 0)
    def _():
        xn_ref[...] = _rms(x_ref[...], g_ref[...]).astype(BF16)
        acc_ref[...] = jnp.zeros_like(acc_ref)

    xn = xn_ref[...]
    gate = jnp.dot(xn, wg_ref[...], preferred_element_type=F32)
    up = jnp.dot(xn, wu_ref[...], preferred_element_type=F32)
    mid = (gate * _sigmoid(gate) * up).astype(BF16)
    acc_ref[...] += jnp.dot(mid, wd_ref[...], preferred_element_type=F32)

    @pl.when(j == nj - 1)
    def _():
        h = x_ref[...] + 0.5 * acc_ref[...]
        if mode == "emit":
            o_ref[...] = h
            u_ref[...] = _rms(h, g2_ref[...]).astype(BF16)
        else:
            o_ref[...] = _rms(h, g2_ref[...])


def _ffn(x, g, wg, wu, wd, g2, mode, tm=512, tf=512):
    t, d = x.shape
    f = wg.shape[1]
    nj = f // tf
    out_shape = [jax.ShapeDtypeStruct((t, d), F32)]
    out_specs = [pl.BlockSpec((tm, d), lambda i, j: (i, 0))]
    if mode == "emit":
        out_shape.append(jax.ShapeDtypeStruct((t, d), BF16))
        out_specs.append(pl.BlockSpec((tm, d), lambda i, j: (i, 0)))
    res = pl.pallas_call(
        functools.partial(_ffn_kernel, nj=nj, mode=mode),
        grid=(t // tm, nj),
        in_specs=[
            pl.BlockSpec((tm, d), lambda i, j: (i, 0)),
            pl.BlockSpec((1, d), lambda i, j: (0, 0)),
            pl.BlockSpec((d, tf), lambda i, j: (0, j)),
            pl.BlockSpec((d, tf), lambda i, j: (0, j)),
            pl.BlockSpec((tf, d), lambda i, j: (j, 0)),
            pl.BlockSpec((1, d), lambda i, j: (0, 0)),
        ],
        out_specs=out_specs,
        out_shape=out_shape,
        scratch_shapes=[pltpu.VMEM((tm, d), BF16), pltpu.VMEM((tm, d), F32)],
        compiler_params=pltpu.CompilerParams(
            dimension_semantics=("arbitrary", "arbitrary"), vmem_limit_bytes=VMEM_LIMIT),
        name="ffn_" + mode,
    )(x, g, wg, wu, wd, g2)
    return res


def _rwkv_kernel(u_ref, w_ref, mu_ref, w0_ref, w2_ref, a0_ref, a2_ref, g2_ref, kk_ref, ka_ref, rk_ref,
                 lnw_ref, lnb_ref, o_ref,
                 carry_ref, st_ref, at_s, rt_s, bt_s, kt_s, v_s, y_s, bonus_s, g_s,
                 *, tb):
    @pl.when(pl.program_id(0) == 0)
    def _():
        carry_ref[...] = jnp.zeros_like(carry_ref)
        st_ref[...] = jnp.zeros_like(st_ref)

    u = u_ref[...]

    def mixed(c0, c1):
        cols = jnp.dot(u, w_ref[:, c0:c1], preferred_element_type=F32)
        prev = _shift_rows(carry_ref[:, c0:c1], cols, 1)
        carry_ref[:, c0:c1] = cols[tb - SUBLANES:]
        return cols + (prev - cols) * mu_ref[:, c0:c1]

    xl = mixed(RW_OFF_WL, RW_PAD_COLS)
    wl = xl[:, :LANES]
    al = xl[:, LANES:2 * LANES]
    gl = xl[:, 2 * LANES:]
    zw = w0_ref[...] + _dot(jnp.tanh(wl), w2_ref[...])
    w_log = -_softplus(-zw) - 0.5
    lw = -jnp.exp(w_log)
    a = _sigmoid(a0_ref[...] + _dot(al, a2_ref[...]))
    g_s[...] = _dot(_sigmoid(gl), g2_ref[...])
    c = _mm(_chunk_prefix_mask(tb), lw, "nn", 1, 2)
    einv = jnp.exp(-c)

    k = mixed(RW_WIDTH, 2 * RW_WIDTH)
    kkv = k * kk_ref[...]
    kkn = kkv * lax.rsqrt(_group_sum(kkv * kkv, RW_HEAD_DIM) + 1e-6)
    k2 = k * (1.0 + (a - 1.0) * ka_ref[...])
    at_s[...] = (-kkn * jnp.exp(c - lw)).astype(BF16)
    bt_s[...] = (kkn * a * einv).astype(BF16)
    kt_s[...] = (k2 * einv).astype(BF16)

    r = mixed(0, RW_WIDTH)
    rt_s[...] = (r * jnp.exp(c)).astype(BF16)
    rkk = _group_sum(r * k2 * rk_ref[...], RW_HEAD_DIM)

    v = mixed(2 * RW_WIDTH, 3 * RW_WIDTH)
    v_s[...] = v.astype(BF16)
    bonus_s[...] = rkk * v

    L = CHUNK
    P2 = 2 * L
    n_pairs = RW_HEADS // 2
    lane_l = _iota((L, P2), 1)
    row_l = _iota((L, P2), 0)
    h0_l = lane_l < L
    strict_l = row_l > (lane_l & (L - 1))
    incl_l = row_l >= (lane_l & (L - 1))
    rr = _iota((P2, P2), 0)
    cc = _iota((P2, P2), 1)
    same_blk = ((rr ^ cc) & L) == 0
    strict_bd = same_blk & ((rr & (L - 1)) > (cc & (L - 1)))
    top = rr < L

    def only0(z):
        return jnp.where(h0_l, z, jnp.zeros_like(z))

    def only1(z):
        return jnp.where(h0_l, jnp.zeros_like(z), z)

    n_chunks = tb // L
    sls = [slice(p * P2, (p + 1) * P2) for p in range(n_pairs)]
    items = [(ci, p) for ci in range(n_chunks) for p in range(n_pairs)]

    def ld(ref, it):
        return ref[it[0] * L:(it[0] + 1) * L, sls[it[1]]]

    g0 = [_dot(jnp.concatenate([only0(ld(at_s, it)), only0(ld(rt_s, it))], 0),
               jnp.concatenate([ld(bt_s, it), ld(kt_s, it)], 0), "nt") for it in items]
    g1 = [_dot(jnp.concatenate([only1(ld(rt_s, it)), only1(ld(at_s, it))], 0),
               jnp.concatenate([ld(kt_s, it), ld(bt_s, it)], 0), "nt") for it in items]
    inv = _unit_lower_inverses(
        [jnp.where(strict_bd, jnp.where(top, a0_, a1_), 0.0) for a0_, a1_ in zip(g0, g1)])
    inv = [z.astype(BF16) for z in inv]
    akv = [_dot(jnp.where(strict_l, jnp.where(h0_l, a1_[L:], a0_[:L]), 0.0),
                jnp.concatenate([only1(ld(v_s, it)), only0(ld(v_s, it))], 0))
           for it, a0_, a1_ in zip(items, g0, g1)]
    ly = [jnp.concatenate([jnp.where(incl_l, a0_[L:], 0.0), jnp.where(incl_l, a1_[:L], 0.0)], 1).astype(BF16)
          for a0_, a1_ in zip(g0, g1)]

    state = [st_ref[p] for p in range(n_pairs)]
    for ci in range(n_chunks):
        rows = slice(ci * L, (ci + 1) * L)
        its = [(ci, p) for p in range(n_pairs)]
        k0 = ci * n_pairs
        xs = [_dot(jnp.concatenate([ld(at_s, it), ld(rt_s, it)], 0), state[p], "nt")
              for p, it in enumerate(its)]
        rhs = [xs[p][:L] + akv[k0 + p] for p in range(n_pairs)]
        pp = [_dot(inv[k0 + p], jnp.concatenate([only0(rhs[p]), only1(rhs[p])], 0)) for p in range(n_pairs)]
        pm = [(z[:L] + z[L:]).astype(BF16) for z in pp]
        for p, it in enumerate(its):
            vv = ld(v_s, it)
            ry = jnp.concatenate([only0(pm[p]), only0(vv), only1(vv), only1(pm[p])], 0)
            y_s[rows, sls[p]] = xs[p][L:] + _dot(ly[k0 + p], ry)
        for p, it in enumerate(its):
            upd = _dot(jnp.concatenate([pm[p], ld(v_s, it)], 0),
                       jnp.concatenate([ld(bt_s, it), ld(kt_s, it)], 0), "tn")
            w_end = jnp.exp(c[(ci + 1) * L - 1:(ci + 1) * L, sls[p]])
            state[p] = (state[p] + jnp.where(same_blk, upd, 0.0)) * w_end
    for p in range(n_pairs):
        st_ref[p] = state[p]

    y = y_s[...]
    inv_n = 1.0 / RW_HEAD_DIM
    mu = _group_sum(y, RW_HEAD_DIM) * inv_n
    yc = y - mu
    var = _group_sum(yc * yc, RW_HEAD_DIM) * inv_n
    yn = yc * lax.rsqrt(var + RW_GN_EPS) * lnw_ref[...] + lnb_ref[...]
    o_ref[...] = ((yn + bonus_s[...]) * g_s[...]).astype(o_ref.dtype)


def _resident(a):
    return pl.BlockSpec(a.shape, lambda i: (0,) * a.ndim, pipeline_mode=pl.Buffered(1))


def _rwkv(u, w_rw, mu, w0, w2, a0, a2, g2, k_k, k_a, r_k, ln_w, ln_b, tb=256):
    t, d = u.shape
    tb = min(tb, t)
    w = RW_WIDTH

    def full(a):
        return pl.BlockSpec(a.shape, lambda i: (0,) * a.ndim)

    params = (mu, w0, w2, a0, a2, g2, k_k, k_a, r_k, ln_w, ln_b)
    return pl.pallas_call(
        functools.partial(_rwkv_kernel, tb=tb),
        grid=(t // tb,),
        in_specs=[pl.BlockSpec((tb, d), lambda i: (i, 0)), _resident(w_rw)] + [full(a) for a in params],
        out_specs=pl.BlockSpec((tb, w), lambda i: (i, 0)),
        out_shape=jax.ShapeDtypeStruct((t, w), BF16),
        scratch_shapes=[pltpu.VMEM((SUBLANES, RW_PAD_COLS), F32),
                        pltpu.VMEM((RW_HEADS // 2, 2 * CHUNK, 2 * CHUNK), F32)]
        + [pltpu.VMEM((tb, w), BF16)] * 5 + [pltpu.VMEM((tb, w), F32)] * 3,
        compiler_params=pltpu.CompilerParams(
            dimension_semantics=("arbitrary",), vmem_limit_bytes=VMEM_LIMIT),
        name="rwkv7",
    )(u, w_rw, *params)


def _gdn_kernel(u_ref, w_ref, cw_ref, alog_ref, dtb_ref, nw_ref, o_ref,
                carry_ref, st_ref, q_s, k_s, vb_s, kbe_s, qe_s, kh_s, o_s, *, tb):
    @pl.when(pl.program_id(0) == 0)
    def _():
        carry_ref[...] = jnp.zeros_like(carry_ref)
        st_ref[...] = jnp.zeros_like(st_ref)

    u = u_ref[...]
    cw = cw_ref[...]

    def conv_silu(c0, c1):
        xin = jnp.dot(u, w_ref[:, c0:c1], preferred_element_type=F32)
        carry = carry_ref[:, c0:c1]
        acc = xin * cw[GDN_CONV - 1:GDN_CONV, c0:c1]
        for j in range(1, GDN_CONV):
            acc = acc + _shift_rows(carry, xin, j) * cw[GDN_CONV - 1 - j:GDN_CONV - j, c0:c1]
        carry_ref[:, c0:c1] = xin[tb - SUBLANES:]
        return acc * _sigmoid(acc)

    def l2n(z):
        return z * lax.rsqrt(jnp.sum(z * z, axis=-1, keepdims=True) + 1e-6)

    hd = GDN_HEAD_DIM
    ba = jnp.dot(u, w_ref[:, GDN_OFF_BA:GDN_PAD_COLS], preferred_element_type=F32)
    beta = _sigmoid(ba)
    gstep = -jnp.exp(alog_ref[...]) * _softplus(ba + dtb_ref[...])
    gc = _mm(_chunk_prefix_mask(tb), gstep, "nn", 1, 3)
    hh = _iota((LANES, GDN_V_WIDTH), 0)
    blk = lax.shift_right_logical(_iota((LANES, GDN_V_WIDTH), 1), hd.bit_length() - 1)
    e_beta = _ones_where(hh == blk)
    e_g = _ones_where(hh == blk + GDN_V_HEADS)
    bx = _mm(beta, e_beta, "nn", 2, 1)
    gcx = _mm(gc, e_g, "nn", 3, 1)
    egc = jnp.exp(gcx)
    L = CHUNK
    rep = GDN_V_HEADS // GDN_QK_HEADS
    hsl = [slice(h * hd, (h + 1) * hd) for h in range(GDN_V_HEADS)]

    kc = conv_silu(GDN_QK_WIDTH, 2 * GDN_QK_WIDTH)
    for h in range(GDN_V_HEADS):
        sl = hsl[h]
        if h % rep == 0:
            kn = l2n(kc[:, hsl[h // rep]])
            k_s[:, hsl[h // rep]] = kn.astype(BF16)
        kbe_s[:, sl] = (kn * bx[:, sl] * egc[:, sl]).astype(BF16)
        for ci in range(tb // L):
            rs = slice(ci * L, (ci + 1) * L)
            to_end = jnp.exp(gcx[(ci + 1) * L - 1:(ci + 1) * L, sl] - gcx[rs, sl])
            kh_s[rs, sl] = (kn[rs] * to_end).astype(BF16)

    qc = conv_silu(0, GDN_QK_WIDTH)
    for h in range(GDN_V_HEADS):
        if h % rep == 0:
            qn = l2n(qc[:, hsl[h // rep]]) * (hd ** -0.5)
            q_s[:, hsl[h // rep]] = qn.astype(BF16)
        qe_s[:, hsl[h]] = (qn * egc[:, hsl[h]]).astype(BF16)

    vb_s[...] = (conv_silu(2 * GDN_QK_WIDTH, GDN_CONV_CH) * bx).astype(BF16)

    P2 = 2 * L
    rr = _iota((P2, P2), 0)
    cc = _iota((P2, P2), 1)
    same_blk = ((rr ^ cc) & L) == 0
    lr = rr & (L - 1)
    lc = cc & (L - 1)
    strict_bd = same_blk & (lr > lc)
    incl_bd = same_blk & (lr >= lc)
    n_pairs = GDN_QK_HEADS

    n_chunks = tb // L
    items = [(ci, p) for ci in range(n_chunks) for p in range(n_pairs)]
    gm = []
    for ci, p in items:
        rows = slice(ci * L, (ci + 1) * L)
        kp = k_s[rows, hsl[p]]
        gm.append(_dot(jnp.concatenate([kp, q_s[rows, hsl[p]]], 0),
                       jnp.concatenate([kp, kp], 0), "nt"))
    a_bd = []
    a_qk = []
    for (ci, p), g in zip(items, gm):
        rows = slice(ci * L, (ci + 1) * L)
        h0, h1 = hsl[2 * p], hsl[2 * p + 1]
        gcol = jnp.concatenate([gcx[rows, h0], gcx[rows, h1]], 0)
        diff = gcol - gcol.T
        decay = jnp.where(incl_bd, jnp.exp(jnp.where(incl_bd, diff, 0.0)), 0.0)
        bcol = jnp.concatenate([bx[rows, h0], bx[rows, h1]], 0)
        a_bd.append(jnp.where(strict_bd, -(jnp.concatenate([g[:L], g[:L]], 0) * bcol * decay), 0.0))
        a_qk.append((jnp.concatenate([g[L:], g[L:]], 0) * decay).astype(BF16))
    inv = _unit_lower_inverses(a_bd)
    uw = []
    for (ci, p), iv in zip(items, inv):
        rows = slice(ci * L, (ci + 1) * L)
        h0, h1 = hsl[2 * p], hsl[2 * p + 1]
        rhs = jnp.concatenate([jnp.concatenate([vb_s[rows, h0], kbe_s[rows, h0]], 1),
                               jnp.concatenate([vb_s[rows, h1], kbe_s[rows, h1]], 1)], 0)
        uw.append(_dot(iv, rhs))

    state = [st_ref[h] for h in range(GDN_V_HEADS)]
    for ci in range(n_chunks):
        rows = slice(ci * L, (ci + 1) * L)
        x = []
        for h in range(GDN_V_HEADS):
            w_h = uw[ci * n_pairs + h // 2][(h % 2) * L:(h % 2 + 1) * L, hd:]
            x.append(_dot(jnp.concatenate([w_h.astype(BF16), qe_s[rows, hsl[h]]], 0), state[h]))
        vnew = [(uw[ci * n_pairs + h // 2][(h % 2) * L:(h % 2 + 1) * L, :hd] - x[h][:L]).astype(BF16)
                for h in range(GDN_V_HEADS)]
        for p in range(n_pairs):
            o_pair = _dot(a_qk[ci * n_pairs + p], jnp.concatenate([vnew[2 * p], vnew[2 * p + 1]], 0))
            o_s[rows, hsl[2 * p]] = x[2 * p][L:] + o_pair[:L]
            o_s[rows, hsl[2 * p + 1]] = x[2 * p + 1][L:] + o_pair[L:]
        state = [state[h] * jnp.exp(gcx[(ci + 1) * L - 1:(ci + 1) * L, hsl[h]])
                 + _dot(kh_s[rows, hsl[h]], vnew[h], "tn") for h in range(GDN_V_HEADS)]
    for h in range(GDN_V_HEADS):
        st_ref[h] = state[h]

    nw = nw_ref[...]
    for h in range(GDN_V_HEADS):
        sl = hsl[h]
        zz = jnp.dot(u, w_ref[:, GDN_OFF_Z + h * hd:GDN_OFF_Z + (h + 1) * hd], preferred_element_type=F32)
        o = o_s[:, sl]
        on = o * lax.rsqrt(jnp.mean(o * o, axis=-1, keepdims=True) + RMS_EPS) * nw
        o_ref[:, sl] = (on * (zz * _sigmoid(zz))).astype(o_ref.dtype)


def _gdn(u, w_gdn, conv_w, alog_pad, dtb_pad, norm_w, tb=256):
    t, d = u.shape
    tb = min(tb, t)

    def full(a):
        return pl.BlockSpec(a.shape, lambda i: (0,) * a.ndim)

    params = (conv_w, alog_pad, dtb_pad, norm_w)
    qk = pltpu.VMEM((tb, GDN_QK_WIDTH), BF16)
    return pl.pallas_call(
        functools.partial(_gdn_kernel, tb=tb),
        grid=(t // tb,),
        in_specs=[pl.BlockSpec((tb, d), lambda i: (i, 0)), _resident(w_gdn)] + [full(a) for a in params],
        out_specs=pl.BlockSpec((tb, GDN_V_WIDTH), lambda i: (i, 0)),
        out_shape=jax.ShapeDtypeStruct((t, GDN_V_WIDTH), BF16),
        scratch_shapes=[pltpu.VMEM((SUBLANES, GDN_CONV_CH), F32),
                        pltpu.VMEM((GDN_V_HEADS, GDN_HEAD_DIM, GDN_HEAD_DIM), F32),
                        qk, qk]
        + [pltpu.VMEM((tb, GDN_V_WIDTH), BF16)] * 4 + [pltpu.VMEM((tb, GDN_V_WIDTH), F32)],
        compiler_params=pltpu.CompilerParams(
            dimension_semantics=("arbitrary",), vmem_limit_bytes=VMEM_LIMIT),
        name="gdn",
    )(u, w_gdn, *params)


def _merge_kernel(u_ref, ya_ref, yb_ref, wga_ref, wgb_ref, la_ref, lb_ref, o_ref):
    u = u_ref[...]
    ga = _sigmoid(jnp.dot(u, wga_ref[...], preferred_element_type=F32))
    gb = _sigmoid(jnp.dot(u, wgb_ref[...], preferred_element_type=F32))
    ma = jnp.dot(ya_ref[...], la_ref[...], preferred_element_type=F32)
    mb = jnp.dot(yb_ref[...], lb_ref[...], preferred_element_type=F32)
    o_ref[...] = (ga * ma + gb * mb).astype(o_ref.dtype)


def _merge(u, ya, yb, wga, wgb, la, lb, tm=1024, tn=512):
    t, d = u.shape
    tm = min(tm, t)
    bw = ya.shape[1]
    return pl.pallas_call(
        _merge_kernel,
        grid=(d // tn, t // tm),
        in_specs=[
            pl.BlockSpec((tm, d), lambda c, i: (i, 0)),
            pl.BlockSpec((tm, bw), lambda c, i: (i, 0)),
            pl.BlockSpec((tm, bw), lambda c, i: (i, 0)),
            pl.BlockSpec((d, tn), lambda c, i: (0, c)),
            pl.BlockSpec((d, tn), lambda c, i: (0, c)),
            pl.BlockSpec((bw, tn), lambda c, i: (0, c)),
            pl.BlockSpec((bw, tn), lambda c, i: (0, c)),
        ],
        out_specs=pl.BlockSpec((tm, tn), lambda c, i: (i, c)),
        out_shape=jax.ShapeDtypeStruct((t, d), BF16),
        compiler_params=pltpu.CompilerParams(
            dimension_semantics=("arbitrary", "arbitrary"), vmem_limit_bytes=VMEM_LIMIT),
        name="merge",
    )(u, ya, yb, wga, wgb, la, lb)


def _resid_kernel(x_ref, w_ref, r_ref, g_ref, h_ref, *rest, emit):
    h = r_ref[...] + jnp.dot(x_ref[...], w_ref[...], preferred_element_type=F32)
    h_ref[...] = h
    if emit:
        rest[0][...] = _rms(h, g_ref[...]).astype(BF16)


def _resid_proj(x, w, resid, g, emit, name, tm=512):
    t, k = x.shape
    d = w.shape[1]
    tm = min(tm, t)
    out_shape = [jax.ShapeDtypeStruct((t, d), F32)]
    out_specs = [pl.BlockSpec((tm, d), lambda i: (i, 0))]
    if emit:
        out_shape.append(jax.ShapeDtypeStruct((t, d), BF16))
        out_specs.append(pl.BlockSpec((tm, d), lambda i: (i, 0)))
    return pl.pallas_call(
        functools.partial(_resid_kernel, emit=emit),
        grid=(t // tm,),
        in_specs=[
            pl.BlockSpec((tm, k), lambda i: (i, 0)),
            pl.BlockSpec((k, d), lambda i: (0, 0)),
            pl.BlockSpec((tm, d), lambda i: (i, 0)),
            pl.BlockSpec((1, d), lambda i: (0, 0)),
        ],
        out_specs=out_specs,
        out_shape=out_shape,
        compiler_params=pltpu.CompilerParams(
            dimension_semantics=("arbitrary",), vmem_limit_bytes=VMEM_LIMIT),
        name=name,
    )(x, w, resid, g)


def _memkv_kernel(m_ref, g_ref, wk_ref, wv_ref, k_ref, v_ref):
    mn = _rms(m_ref[...], g_ref[...]).astype(BF16)
    k_ref[...] = jnp.dot(mn, wk_ref[...], preferred_element_type=F32).astype(BF16)
    v_ref[...] = jnp.dot(mn, wv_ref[...], preferred_element_type=F32).astype(BF16)


def _memkv(mem, g, wk, wv, tn=512):
    m, d = mem.shape
    return pl.pallas_call(
        _memkv_kernel,
        grid=(d // tn,),
        in_specs=[
            pl.BlockSpec((m, d), lambda c: (0, 0)),
            pl.BlockSpec((1, d), lambda c: (0, 0)),
            pl.BlockSpec((d, tn), lambda c: (0, c)),
            pl.BlockSpec((d, tn), lambda c: (0, c)),
        ],
        out_specs=[pl.BlockSpec((m, tn), lambda c: (0, c))] * 2,
        out_shape=[jax.ShapeDtypeStruct((m, d), BF16)] * 2,
        compiler_params=pltpu.CompilerParams(
            dimension_semantics=("arbitrary",), vmem_limit_bytes=VMEM_LIMIT),
        name="memkv",
    )(mem, g, wk, wv)


def _xattn_kernel(hq_ref, wq_ref, k_ref, v_ref, o_ref):
    q = jnp.dot(hq_ref[...], wq_ref[...], preferred_element_type=F32)
    hd = MEM_HEAD_DIM
    for h in range(MEM_HEADS):
        sl = slice(h * hd, (h + 1) * hd)
        s = _mm(q[:, sl].astype(BF16), k_ref[:, sl], "nt") * (hd ** -0.5)
        e = jnp.exp(s - jnp.max(s, axis=-1, keepdims=True))
        p = e / jnp.sum(e, axis=-1, keepdims=True)
        o_ref[:, sl] = jnp.dot(p.astype(BF16), v_ref[:, sl], preferred_element_type=F32).astype(BF16)


def _xattn(hq, wq, k, v, tm=512):
    t, d = hq.shape
    tm = min(tm, t)
    m = k.shape[0]
    return pl.pallas_call(
        _xattn_kernel,
        grid=(t // tm,),
        in_specs=[
            pl.BlockSpec((tm, d), lambda i: (i, 0)),
            pl.BlockSpec((d, d), lambda i: (0, 0)),
            pl.BlockSpec((m, d), lambda i: (0, 0)),
            pl.BlockSpec((m, d), lambda i: (0, 0)),
        ],
        out_specs=pl.BlockSpec((tm, d), lambda i: (i, 0)),
        out_shape=jax.ShapeDtypeStruct((t, d), BF16),
        compiler_params=pltpu.CompilerParams(
            dimension_semantics=("arbitrary",), vmem_limit_bytes=VMEM_LIMIT),
        name="xattn",
    )(hq, wq, k, v)


def _pad_cols(a, n):
    return jnp.pad(a, ((0, 0), (0, n - a.shape[1])))


def _pad_rows(a, n):
    return jnp.pad(a, ((0, n - a.shape[0]), (0, 0)))


def _rw_layout(a):
    o1 = 3 * RW_WIDTH
    o2 = o1 + RW_DECAY_LORA
    o3 = o2 + RW_AAA_LORA
    return jnp.concatenate([
        a[:, :o1],
        _pad_cols(a[:, o1:o2], LANES),
        _pad_cols(a[:, o2:o3], LANES),
        _pad_cols(a[:, o3:], 2 * LANES),
    ], axis=1)


def kernel(x, mem, ffn1_norm, ffn1_w_gate, ffn1_w_up, ffn1_w_down, mix_norm, w_in, rw_shift_mu, rw_w0, rw_w2, rw_a0, rw_a2, rw_g2, rw_k_k, rw_k_a, rw_r_k, rw_ln_w, rw_ln_b, gdn_conv_w, gdn_a_log, gdn_dt_bias, gdn_norm_w, w_lift, w_out, mem_q_norm, mem_kv_norm, w_mq, w_mk, w_mv, w_mo, ffn2_norm, ffn2_w_gate, ffn2_w_up, ffn2_w_down, final_norm):
    b_, t, d = x.shape
    assert b_ == 1 and d == D_MODEL and ffn1_norm.shape[0] == 1
    l = 0
    bf = lambda a: a.astype(BF16)
    row = lambda a: a.reshape(1, -1)

    h0 = x.reshape(t, d)
    h1, u = _ffn(h0, row(ffn1_norm[l]), bf(ffn1_w_gate[l]), bf(ffn1_w_up[l]), bf(ffn1_w_down[l]),
                 row(mix_norm[l]), "emit")

    w_in_l = bf(w_in[l])
    w_rw = _rw_layout(w_in_l[:, :RW_COLS])
    w_gdn = _pad_cols(w_in_l[:, RW_COLS:RW_COLS + GDN_COLS], GDN_PAD_COLS)
    w_gate = w_in_l[:, RW_COLS + GDN_COLS:]
    wga = w_gate[:, :D_MODEL]
    wgb = w_gate[:, D_MODEL:]

    y_rw = _rwkv(
        u, w_rw, _rw_layout(row(rw_shift_mu[l])), row(rw_w0[l]), _pad_rows(rw_w2[l], LANES),
        row(rw_a0[l]), _pad_rows(rw_a2[l], LANES), _pad_rows(rw_g2[l], 2 * LANES),
        row(rw_k_k[l]), row(rw_k_a[l]), row(rw_r_k[l]), row(rw_ln_w[l]), row(rw_ln_b[l]))

    nh = GDN_V_HEADS
    alog_pad = jnp.pad(row(gdn_a_log[l]), ((0, 0), (nh, LANES - 2 * nh)))
    dtb_pad = jnp.pad(row(gdn_dt_bias[l]), ((0, 0), (nh, LANES - 2 * nh)))
    y_gdn = _gdn(u, w_gdn, gdn_conv_w[l], alog_pad, dtb_pad, row(gdn_norm_w[l]))

    merged = _merge(u, y_rw, y_gdn, wga, wgb, bf(w_lift[l, 0]), bf(w_lift[l, 1]))
    h2, hq = _resid_proj(merged, bf(w_out[l]), h1, row(mem_q_norm[l]), True, "out_proj")

    mk, mv = _memkv(mem.reshape(mem.shape[1], d), row(mem_kv_norm[l]), bf(w_mk[l]), bf(w_mv[l]))
    att = _xattn(hq, bf(w_mq[l]), mk, mv)
    (h3,) = _resid_proj(att, bf(w_mo[l]), h2, row(mem_q_norm[l]), False, "attn_out")

    (y,) = _ffn(h3, row(ffn2_norm[l]), bf(ffn2_w_gate[l]), bf(ffn2_w_up[l]), bf(ffn2_w_down[l]),
                row(final_norm), "final")
    return y.reshape(b_, t, d)
```

```python
import functools

import jax
import jax.numpy as jnp
from jax import lax
from jax.experimental import pallas as pl
from jax.experimental.pallas import tpu as pltpu

F32 = jnp.float32
BF16 = jnp.bfloat16

D_MODEL = 2048
D_FF = 5632
RMS_EPS = 1e-6

RW_HEADS = 16
RW_HEAD_DIM = 64
RW_WIDTH = 1024
RW_DECAY_LORA = 64
RW_AAA_LORA = 64
RW_GATE_LORA = 160
RW_GN_EPS = 64e-5
RW_COLS = 3 * RW_WIDTH + RW_DECAY_LORA + RW_AAA_LORA + RW_GATE_LORA
LANES = 128
RW_OFF_WL = 3 * RW_WIDTH
RW_OFF_AL = RW_OFF_WL + LANES
RW_OFF_GL = RW_OFF_AL + LANES
RW_PAD_COLS = RW_OFF_GL + 2 * LANES

GDN_QK_HEADS = 4
GDN_V_HEADS = 8
GDN_HEAD_DIM = 128
GDN_QK_WIDTH = 512
GDN_V_WIDTH = 1024
GDN_CONV = 4
GDN_CONV_CH = 2 * GDN_QK_WIDTH + GDN_V_WIDTH
GDN_COLS = GDN_CONV_CH + GDN_V_WIDTH + 2 * GDN_V_HEADS
GDN_OFF_Z = GDN_CONV_CH
GDN_OFF_BA = GDN_CONV_CH + GDN_V_WIDTH
GDN_PAD_COLS = GDN_OFF_BA + LANES

MEM_HEADS = 4
MEM_HEAD_DIM = D_MODEL // MEM_HEADS

CHUNK = 64
SUBLANES = 8
VMEM_LIMIT = 56 * 1024 * 1024

_DIMS = {
    "nn": (((1,), (0,)), ((), ())),
    "nt": (((1,), (1,)), ((), ())),
    "tn": (((0,), (0,)), ((), ())),
}


def _split_bf16(x, n):
    if x.dtype == BF16:
        return [x]
    parts = []
    r = x
    for i in range(n):
        p = r.astype(BF16)
        parts.append(p)
        if i + 1 < n:
            r = r - p.astype(F32)
    return parts


def _mm(a, b, dims="nn", pa=1, pb=1):
    ap = _split_bf16(a, pa)
    bp = _split_bf16(b, pb)
    n = max(len(ap), len(bp))
    acc = None
    for i, ai in enumerate(ap):
        for j, bj in enumerate(bp):
            if i + j < n:
                t = lax.dot_general(ai, bj, _DIMS[dims], preferred_element_type=F32)
                acc = t if acc is None else acc + t
    return acc


def _rms(x, g, eps=RMS_EPS):
    ms = jnp.mean(x * x, axis=-1, keepdims=True)
    return x * lax.rsqrt(ms + eps) * g


def _sigmoid(x):
    return 1.0 / (1.0 + jnp.exp(-x))


def _softplus(x):
    return jnp.maximum(x, 0.0) + jnp.log(1.0 + jnp.exp(-jnp.abs(x)))


def _iota(shape, axis):
    return lax.broadcasted_iota(jnp.int32, shape, axis)


def _ones_where(cond):
    return jnp.where(cond, 1.0, 0.0).astype(BF16)


def _dot(a, b, dims="nn"):
    return lax.dot_general(a.astype(BF16), b.astype(BF16), _DIMS[dims], preferred_element_type=F32)


def _group_sum(x, group):
    shift = group.bit_length() - 1
    wide = 2 * LANES
    gi = lax.shift_right_logical(_iota((wide, wide), 0), shift)
    gj = lax.shift_right_logical(_iota((wide, wide), 1), shift)
    bd = _ones_where(gi == gj)
    xb = x.astype(BF16)
    outs = [jnp.dot(xb[:, c * wide:(c + 1) * wide], bd, preferred_element_type=F32)
            for c in range(x.shape[-1] // wide)]
    return outs[0] if len(outs) == 1 else jnp.concatenate(outs, axis=-1)


def _chunk_prefix_mask(tb):
    shift = CHUNK.bit_length() - 1
    i = _iota((tb, tb), 0)
    j = _iota((tb, tb), 1)
    same = lax.shift_right_logical(i, shift) == lax.shift_right_logical(j, shift)
    return _ones_where(same & (i >= j))


def _shift_rows(carry, x, j):
    xx = jnp.concatenate([carry, x], axis=0)
    return pltpu.roll(xx, j, 0)[SUBLANES:]


def _unit_lower_inverses(mats, out):
    n = mats[0].shape[0]
    eye = jnp.where(_iota((n, n), 0) == _iota((n, n), 1), 1.0, 0.0)
    invs = [eye + a for a in mats]
    pws = [a.astype(BF16) for a in mats]
    for _ in range(CHUNK.bit_length() - 2):
        pws = [_dot(pw, pw).astype(BF16) for pw in pws]
        yield
        invs = [inv + _dot(inv, pw) for inv, pw in zip(invs, pws)]
        yield
    out.extend(invs)


def _run_stages(*gens):
    live = list(gens)
    while live:
        for g in list(live):
            try:
                next(g)
            except StopIteration:
                live.remove(g)


def _chain_stages(gens):
    for g in gens:
        yield from g


def _swiglu_down(xn, wg_ref, wu_ref, wd_ref):
    gate = jnp.dot(xn, wg_ref[...], preferred_element_type=F32)
    up = jnp.dot(xn, wu_ref[...], preferred_element_type=F32)
    mid = (gate * _sigmoid(gate) * up).astype(BF16)
    return jnp.dot(mid, wd_ref[...], preferred_element_type=F32)


def _ffn_emit_kernel(x_ref, g_ref, wg_ref, wu_ref, wd_ref, g2_ref, h_ref, u_ref, xn_ref, acc_ref, *, nj):
    j = pl.program_id(1)

    @pl.when(j == 0)
    def _():
        xn_ref[...] = _rms(x_ref[...], g_ref[...]).astype(BF16)
        acc_ref[...] = jnp.zeros_like(acc_ref)

    acc_ref[...] += _swiglu_down(xn_ref[...], wg_ref, wu_ref, wd_ref)

    @pl.when(j == nj - 1)
    def _():
        h = x_ref[...] + 0.5 * acc_ref[...]
        h_ref[...] = h
        u_ref[...] = _rms(h, g2_ref[...]).astype(BF16)


def _ffn_emit(x, g, wg, wu, wd, g2, tm=512, tf=512):
    t, d = x.shape
    nj = wg.shape[1] // tf
    row_blk = pl.BlockSpec((tm, d), lambda i, j: (i, 0))
    vec = pl.BlockSpec((1, d), lambda i, j: (0, 0))
    return pl.pallas_call(
        functools.partial(_ffn_emit_kernel, nj=nj),
        grid=(t // tm, nj),
        in_specs=[row_blk, vec,
                  pl.BlockSpec((d, tf), lambda i, j: (0, j)),
                  pl.BlockSpec((d, tf), lambda i, j: (0, j)),
                  pl.BlockSpec((tf, d), lambda i, j: (j, 0)),
                  vec],
        out_specs=[row_blk, row_blk],
        out_shape=[jax.ShapeDtypeStruct((t, d), F32), jax.ShapeDtypeStruct((t, d), BF16)],
        scratch_shapes=[pltpu.VMEM((tm, d), BF16), pltpu.VMEM((tm, d), F32)],
        compiler_params=pltpu.CompilerParams(
            dimension_semantics=("arbitrary", "arbitrary"), vmem_limit_bytes=VMEM_LIMIT),
        name="ffn_emit",
    )(x, g, wg, wu, wd, g2)


def _ffn_final_kernel(x_ref, xn_ref, wg_ref, wu_ref, wd_ref, g2_ref, o_ref, *, nj):
    j = pl.program_id(1)

    @pl.when(j == 0)
    def _():
        o_ref[...] = jnp.zeros_like(o_ref)

    o_ref[...] += _swiglu_down(xn_ref[...], wg_ref, wu_ref, wd_ref)

    @pl.when(j == nj - 1)
    def _():
        o_ref[...] = _rms(x_ref[...] + 0.5 * o_ref[...], g2_ref[...])


def _ffn_final(x, xn, wg, wu, wd, g2, tm=1024, tf=256):
    t, d = x.shape
    tm = min(tm, t)
    nj = wg.shape[1] // tf
    row_blk = pl.BlockSpec((tm, d), lambda i, j: (i, 0))
    return pl.pallas_call(
        functools.partial(_ffn_final_kernel, nj=nj),
        grid=(t // tm, nj),
        in_specs=[row_blk, row_blk,
                  pl.BlockSpec((d, tf), lambda i, j: (0, j)),
                  pl.BlockSpec((d, tf), lambda i, j: (0, j)),
                  pl.BlockSpec((tf, d), lambda i, j: (j, 0)),
                  pl.BlockSpec((1, d), lambda i, j: (0, 0))],
        out_specs=row_blk,
        out_shape=jax.ShapeDtypeStruct((t, d), F32),
        compiler_params=pltpu.CompilerParams(
            dimension_semantics=("arbitrary", "arbitrary"), vmem_limit_bytes=VMEM_LIMIT),
        name="ffn_final",
    )(x, xn, wg, wu, wd, g2)


def _rwkv_kernel(u_ref, w_ref, mu_ref, w0_ref, w2_ref, a0_ref, a2_ref, g2_ref, kk_ref, ka_ref, rk_ref,
                 lnw_ref, lnb_ref, o_ref,
                 carry_ref, st_ref, at_s, rt_s, bt_s, kt_s, v_s, y_s, bonus_s, g_s,
                 *, tb):
    @pl.when(pl.program_id(0) == 0)
    def _():
        carry_ref[...] = jnp.zeros_like(carry_ref)
        st_ref[...] = jnp.zeros_like(st_ref)

    u = u_ref[...]

    def mixed(c0, c1):
        cols = jnp.dot(u, w_ref[:, c0:c1], preferred_element_type=F32)
        prev = _shift_rows(carry_ref[:, c0:c1], cols, 1)
        carry_ref[:, c0:c1] = cols[tb - SUBLANES:]
        return cols + (prev - cols) * mu_ref[:, c0:c1]

    xl = mixed(RW_OFF_WL, RW_PAD_COLS)
    wl = xl[:, :LANES]
    al = xl[:, LANES:2 * LANES]
    gl = xl[:, 2 * LANES:]
    zw = w0_ref[...] + _dot(jnp.tanh(wl), w2_ref[...])
    w_log = -_softplus(-zw) - 0.5
    lw = -jnp.exp(w_log)
    a = _sigmoid(a0_ref[...] + _dot(al, a2_ref[...]))
    g_s[...] = _dot(_sigmoid(gl), g2_ref[...])
    c = _mm(_chunk_prefix_mask(tb), lw, "nn", 1, 2)
    einv = jnp.exp(-c)

    k = mixed(RW_WIDTH, 2 * RW_WIDTH)
    kkv = k * kk_ref[...]
    kkn = kkv * lax.rsqrt(_group_sum(kkv * kkv, RW_HEAD_DIM) + 1e-6)
    k2 = k * (1.0 + (a - 1.0) * ka_ref[...])
    at_s[...] = (-kkn * jnp.exp(c - lw)).astype(BF16)
    bt_s[...] = (kkn * a * einv).astype(BF16)
    kt_s[...] = (k2 * einv).astype(BF16)

    r = mixed(0, RW_WIDTH)
    rt_s[...] = (r * jnp.exp(c)).astype(BF16)
    rkk = _group_sum(r * k2 * rk_ref[...], RW_HEAD_DIM)

    v = mixed(2 * RW_WIDTH, 3 * RW_WIDTH)
    v_s[...] = v.astype(BF16)
    bonus_s[...] = rkk * v

    L = CHUNK
    P2 = 2 * L
    n_pairs = RW_HEADS // 2
    lane_l = _iota((L, P2), 1)
    row_l = _iota((L, P2), 0)
    h0_l = lane_l < L
    strict_l = row_l > (lane_l & (L - 1))
    incl_l = row_l >= (lane_l & (L - 1))
    rr = _iota((P2, P2), 0)
    cc = _iota((P2, P2), 1)
    same_blk = ((rr ^ cc) & L) == 0
    strict_bd = same_blk & ((rr & (L - 1)) > (cc & (L - 1)))
    top = rr < L

    def only0(z):
        return jnp.where(h0_l, z, jnp.zeros_like(z))

    def only1(z):
        return jnp.where(h0_l, jnp.zeros_like(z), z)

    n_chunks = tb // L
    sls = [slice(p * P2, (p + 1) * P2) for p in range(n_pairs)]

    def ld(ref, it):
        return ref[it[0] * L:(it[0] + 1) * L, sls[it[1]]]

    def state_free(chunks, out):
        its = [(ci, p) for ci in chunks for p in range(n_pairs)]
        g0 = [_dot(jnp.concatenate([only0(ld(at_s, it)), only0(ld(rt_s, it))], 0),
                   jnp.concatenate([ld(bt_s, it), ld(kt_s, it)], 0), "nt") for it in its]
        yield
        g1 = [_dot(jnp.concatenate([only1(ld(rt_s, it)), only1(ld(at_s, it))], 0),
                   jnp.concatenate([ld(kt_s, it), ld(bt_s, it)], 0), "nt") for it in its]
        yield
        inv = []
        yield from _unit_lower_inverses(
            [jnp.where(strict_bd, jnp.where(top, a0_, a1_), 0.0) for a0_, a1_ in zip(g0, g1)], inv)
        akv = [_dot(jnp.where(strict_l, jnp.where(h0_l, a1_[L:], a0_[:L]), 0.0),
                    jnp.concatenate([only1(ld(v_s, it)), only0(ld(v_s, it))], 0))
               for it, a0_, a1_ in zip(its, g0, g1)]
        yield
        for k, it in enumerate(its):
            ly = jnp.concatenate([jnp.where(incl_l, g0[k][L:], 0.0), jnp.where(incl_l, g1[k][:L], 0.0)], 1)
            out[it] = (inv[k].astype(BF16), akv[k], ly.astype(BF16))

    state = [st_ref[p] for p in range(n_pairs)]

    def state_tail(ci, pre):
        rows = slice(ci * L, (ci + 1) * L)
        its = [(ci, p) for p in range(n_pairs)]
        xs = [_dot(jnp.concatenate([ld(at_s, it), ld(rt_s, it)], 0), state[p], "nt")
              for p, it in enumerate(its)]
        yield
        rhs = [xs[p][:L] + pre[it][1] for p, it in enumerate(its)]
        pp = [_dot(pre[it][0], jnp.concatenate([only0(rhs[p]), only1(rhs[p])], 0)) for p, it in enumerate(its)]
        yield
        pm = [(z[:L] + z[L:]).astype(BF16) for z in pp]
        for p, it in enumerate(its):
            vv = ld(v_s, it)
            ry = jnp.concatenate([only0(pm[p]), only0(vv), only1(vv), only1(pm[p])], 0)
            y_s[rows, sls[p]] = xs[p][L:] + _dot(pre[it][2], ry)
        yield
        for p, it in enumerate(its):
            upd = _dot(jnp.concatenate([pm[p], ld(v_s, it)], 0),
                       jnp.concatenate([ld(bt_s, it), ld(kt_s, it)], 0), "tn")
            w_end = jnp.exp(c[(ci + 1) * L - 1:(ci + 1) * L, sls[p]])
            state[p] = (state[p] + jnp.where(same_blk, upd, 0.0)) * w_end
        yield

    pre = {}
    first = list(range(max(1, n_chunks - 1)))
    rest = list(range(len(first), n_chunks))
    _run_stages(state_free(first, pre))
    _run_stages(state_free(rest, pre) if rest else iter(()),
                _chain_stages(state_tail(ci, pre) for ci in first))
    _run_stages(_chain_stages(state_tail(ci, pre) for ci in rest))
    for p in range(n_pairs):
        st_ref[p] = state[p]

    y = y_s[...]
    inv_n = 1.0 / RW_HEAD_DIM
    mu = _group_sum(y, RW_HEAD_DIM) * inv_n
    yc = y - mu
    var = _group_sum(yc * yc, RW_HEAD_DIM) * inv_n
    yn = yc * lax.rsqrt(var + RW_GN_EPS) * lnw_ref[...] + lnb_ref[...]
    o_ref[...] = ((yn + bonus_s[...]) * g_s[...]).astype(o_ref.dtype)


def _resident(a):
    return pl.BlockSpec(a.shape, lambda i: (0,) * a.ndim, pipeline_mode=pl.Buffered(1))


def _rwkv(u, w_rw, mu, w0, w2, a0, a2, g2, k_k, k_a, r_k, ln_w, ln_b, tb=256):
    t, d = u.shape
    tb = min(tb, t)
    w = RW_WIDTH

    def full(a):
        return pl.BlockSpec(a.shape, lambda i: (0,) * a.ndim)

    params = (mu, w0, w2, a0, a2, g2, k_k, k_a, r_k, ln_w, ln_b)
    return pl.pallas_call(
        functools.partial(_rwkv_kernel, tb=tb),
        grid=(t // tb,),
        in_specs=[pl.BlockSpec((tb, d), lambda i: (i, 0)), _resident(w_rw)] + [full(a) for a in params],
        out_specs=pl.BlockSpec((tb, w), lambda i: (i, 0)),
        out_shape=jax.ShapeDtypeStruct((t, w), BF16),
        scratch_shapes=[pltpu.VMEM((SUBLANES, RW_PAD_COLS), F32),
                        pltpu.VMEM((RW_HEADS // 2, 2 * CHUNK, 2 * CHUNK), F32)]
        + [pltpu.VMEM((tb, w), BF16)] * 5 + [pltpu.VMEM((tb, w), F32)] * 3,
        compiler_params=pltpu.CompilerParams(
            dimension_semantics=("arbitrary",), vmem_limit_bytes=VMEM_LIMIT),
        name="rwkv7",
    )(u, w_rw, *params)


def _gdn_kernel(u_ref, w_ref, cw_ref, alog_ref, dtb_ref, nw_ref, o_ref,
                carry_ref, st_ref, q_s, k_s, vb_s, kbe_s, qe_s, kh_s, o_s, *, tb):
    @pl.when(pl.program_id(0) == 0)
    def _():
        carry_ref[...] = jnp.zeros_like(carry_ref)
        st_ref[...] = jnp.zeros_like(st_ref)

    u = u_ref[...]
    cw = cw_ref[...]

    def conv_silu(c0, c1):
        xin = jnp.dot(u, w_ref[:, c0:c1], preferred_element_type=F32)
        carry = carry_ref[:, c0:c1]
        acc = xin * cw[GDN_CONV - 1:GDN_CONV, c0:c1]
        for j in range(1, GDN_CONV):
            acc = acc + _shift_rows(carry, xin, j) * cw[GDN_CONV - 1 - j:GDN_CONV - j, c0:c1]
        carry_ref[:, c0:c1] = xin[tb - SUBLANES:]
        return acc * _sigmoid(acc)

    def l2n(z):
        return z * lax.rsqrt(jnp.sum(z * z, axis=-1, keepdims=True) + 1e-6)

    hd = GDN_HEAD_DIM
    ba = jnp.dot(u, w_ref[:, GDN_OFF_BA:GDN_PAD_COLS], preferred_element_type=F32)
    beta = _sigmoid(ba)
    gstep = -jnp.exp(alog_ref[...]) * _softplus(ba + dtb_ref[...])
    gc = _mm(_chunk_prefix_mask(tb), gstep, "nn", 1, 2)
    hh = _iota((LANES, GDN_V_WIDTH), 0)
    blk = lax.shift_right_logical(_iota((LANES, GDN_V_WIDTH), 1), hd.bit_length() - 1)
    e_beta = _ones_where(hh == blk)
    e_g = _ones_where(hh == blk + GDN_V_HEADS)
    bx = _dot(beta, e_beta)
    gcx = _mm(gc, e_g, "nn", 2, 1)
    egc = jnp.exp(gcx)
    L = CHUNK
    rep = GDN_V_HEADS // GDN_QK_HEADS
    hsl = [slice(h * hd, (h + 1) * hd) for h in range(GDN_V_HEADS)]

    kc = conv_silu(GDN_QK_WIDTH, 2 * GDN_QK_WIDTH)
    for h in range(GDN_V_HEADS):
        sl = hsl[h]
        if h % rep == 0:
            kn = l2n(kc[:, hsl[h // rep]])
            k_s[:, hsl[h // rep]] = kn.astype(BF16)
        kbe_s[:, sl] = (kn * bx[:, sl] * egc[:, sl]).astype(BF16)
        for ci in range(tb // L):
            rs = slice(ci * L, (ci + 1) * L)
            to_end = jnp.exp(gcx[(ci + 1) * L - 1:(ci + 1) * L, sl] - gcx[rs, sl])
            kh_s[rs, sl] = (kn[rs] * to_end).astype(BF16)

    qc = conv_silu(0, GDN_QK_WIDTH)
    for h in range(GDN_V_HEADS):
        if h % rep == 0:
            qn = l2n(qc[:, hsl[h // rep]]) * (hd ** -0.5)
            q_s[:, hsl[h // rep]] = qn.astype(BF16)
        qe_s[:, hsl[h]] = (qn * egc[:, hsl[h]]).astype(BF16)

    vb_s[...] = (conv_silu(2 * GDN_QK_WIDTH, GDN_CONV_CH) * bx).astype(BF16)

    P2 = 2 * L
    rr = _iota((P2, P2), 0)
    cc = _iota((P2, P2), 1)
    same_blk = ((rr ^ cc) & L) == 0
    lr = rr & (L - 1)
    lc = cc & (L - 1)
    strict_bd = same_blk & (lr > lc)
    incl_bd = same_blk & (lr >= lc)
    n_pairs = GDN_QK_HEADS

    n_chunks = tb // L

    def state_free(chunks, out):
        its = [(ci, p) for ci in chunks for p in range(n_pairs)]
        gm = []
        for ci, p in its:
            rows = slice(ci * L, (ci + 1) * L)
            kp = k_s[rows, hsl[p]]
            gm.append(_dot(jnp.concatenate([kp, q_s[rows, hsl[p]]], 0),
                           jnp.concatenate([kp, kp], 0), "nt"))
        yield
        a_bd = []
        a_qk = []
        for (ci, p), g in zip(its, gm):
            rows = slice(ci * L, (ci + 1) * L)
            h0, h1 = hsl[2 * p], hsl[2 * p + 1]
            gcol = jnp.concatenate([gcx[rows, h0], gcx[rows, h1]], 0)
            diff = gcol - gcol.T
            decay = jnp.where(incl_bd, jnp.exp(jnp.where(incl_bd, diff, 0.0)), 0.0)
            bcol = jnp.concatenate([bx[rows, h0], bx[rows, h1]], 0)
            a_bd.append(jnp.where(strict_bd, -(jnp.concatenate([g[:L], g[:L]], 0) * bcol * decay), 0.0))
            a_qk.append((jnp.concatenate([g[L:], g[L:]], 0) * decay).astype(BF16))
        inv = []
        yield from _unit_lower_inverses(a_bd, inv)
        for k, (ci, p) in enumerate(its):
            rows = slice(ci * L, (ci + 1) * L)
            h0, h1 = hsl[2 * p], hsl[2 * p + 1]
            rhs = jnp.concatenate([jnp.concatenate([vb_s[rows, h0], kbe_s[rows, h0]], 1),
                                   jnp.concatenate([vb_s[rows, h1], kbe_s[rows, h1]], 1)], 0)
            out[(ci, p)] = (_dot(inv[k], rhs), a_qk[k])
        yield

    state = [st_ref[h] for h in range(GDN_V_HEADS)]

    def state_tail(ci, pre):
        rows = slice(ci * L, (ci + 1) * L)

        def uw_of(h):
            return pre[(ci, h // 2)][0][(h % 2) * L:(h % 2 + 1) * L]

        x = [_dot(jnp.concatenate([uw_of(h)[:, hd:].astype(BF16), qe_s[rows, hsl[h]]], 0), state[h])
             for h in range(GDN_V_HEADS)]
        yield
        vnew = [(uw_of(h)[:, :hd] - x[h][:L]).astype(BF16) for h in range(GDN_V_HEADS)]
        for p in range(n_pairs):
            o_pair = _dot(pre[(ci, p)][1], jnp.concatenate([vnew[2 * p], vnew[2 * p + 1]], 0))
            o_s[rows, hsl[2 * p]] = x[2 * p][L:] + o_pair[:L]
            o_s[rows, hsl[2 * p + 1]] = x[2 * p + 1][L:] + o_pair[L:]
        yield
        for h in range(GDN_V_HEADS):
            state[h] = (state[h] * jnp.exp(gcx[(ci + 1) * L - 1:(ci + 1) * L, hsl[h]])
                        + _dot(kh_s[rows, hsl[h]], vnew[h], "tn"))
        yield

    pre = {}
    first = list(range(max(1, n_chunks - 1)))
    rest = list(range(len(first), n_chunks))
    _run_stages(state_free(first, pre))
    _run_stages(state_free(rest, pre) if rest else iter(()),
                _chain_stages(state_tail(ci, pre) for ci in first))
    _run_stages(_chain_stages(state_tail(ci, pre) for ci in rest))
    for h in range(GDN_V_HEADS):
        st_ref[h] = state[h]

    nw = nw_ref[...]
    for h in range(GDN_V_HEADS):
        sl = hsl[h]
        zz = jnp.dot(u, w_ref[:, GDN_OFF_Z + h * hd:GDN_OFF_Z + (h + 1) * hd], preferred_element_type=F32)
        o = o_s[:, sl]
        on = o * lax.rsqrt(jnp.mean(o * o, axis=-1, keepdims=True) + RMS_EPS) * nw
        o_ref[:, sl] = (on * (zz * _sigmoid(zz))).astype(o_ref.dtype)


def _gdn(u, w_gdn, conv_w, alog_pad, dtb_pad, norm_w, tb=256):
    t, d = u.shape
    tb = min(tb, t)

    def full(a):
        return pl.BlockSpec(a.shape, lambda i: (0,) * a.ndim)

    params = (conv_w, alog_pad, dtb_pad, norm_w)
    qk = pltpu.VMEM((tb, GDN_QK_WIDTH), BF16)
    return pl.pallas_call(
        functools.partial(_gdn_kernel, tb=tb),
        grid=(t // tb,),
        in_specs=[pl.BlockSpec((tb, d), lambda i: (i, 0)), _resident(w_gdn)] + [full(a) for a in params],
        out_specs=pl.BlockSpec((tb, GDN_V_WIDTH), lambda i: (i, 0)),
        out_shape=jax.ShapeDtypeStruct((t, GDN_V_WIDTH), BF16),
        scratch_shapes=[pltpu.VMEM((SUBLANES, GDN_CONV_CH), F32),
                        pltpu.VMEM((GDN_V_HEADS, GDN_HEAD_DIM, GDN_HEAD_DIM), F32),
                        qk, qk]
        + [pltpu.VMEM((tb, GDN_V_WIDTH), BF16)] * 4 + [pltpu.VMEM((tb, GDN_V_WIDTH), F32)],
        compiler_params=pltpu.CompilerParams(
            dimension_semantics=("arbitrary",), vmem_limit_bytes=VMEM_LIMIT),
        name="gdn",
    )(u, w_gdn, *params)


def _merge_kernel(u_ref, ya_ref, yb_ref, wga_ref, wgb_ref, la_ref, lb_ref, o_ref):
    u = u_ref[...]
    ga = _sigmoid(jnp.dot(u, wga_ref[...], preferred_element_type=F32))
    gb = _sigmoid(jnp.dot(u, wgb_ref[...], preferred_element_type=F32))
    ma = jnp.dot(ya_ref[...], la_ref[...], preferred_element_type=F32)
    mb = jnp.dot(yb_ref[...], lb_ref[...], preferred_element_type=F32)
    o_ref[...] = (ga * ma + gb * mb).astype(o_ref.dtype)


def _merge(u, ya, yb, wga, wgb, la, lb, tm=1024, tn=512):
    t, d = u.shape
    tm = min(tm, t)
    bw = ya.shape[1]
    return pl.pallas_call(
        _merge_kernel,
        grid=(d // tn, t // tm),
        in_specs=[
            pl.BlockSpec((tm, d), lambda c, i: (i, 0)),
            pl.BlockSpec((tm, bw), lambda c, i: (i, 0)),
            pl.BlockSpec((tm, bw), lambda c, i: (i, 0)),
            pl.BlockSpec((d, tn), lambda c, i: (0, c)),
            pl.BlockSpec((d, tn), lambda c, i: (0, c)),
            pl.BlockSpec((bw, tn), lambda c, i: (0, c)),
            pl.BlockSpec((bw, tn), lambda c, i: (0, c)),
        ],
        out_specs=pl.BlockSpec((tm, tn), lambda c, i: (i, c)),
        out_shape=jax.ShapeDtypeStruct((t, d), BF16),
        compiler_params=pltpu.CompilerParams(
            dimension_semantics=("arbitrary", "arbitrary"), vmem_limit_bytes=VMEM_LIMIT),
        name="merge",
    )(u, ya, yb, wga, wgb, la, lb)


def _resid_kernel(x_ref, w_ref, r_ref, g_ref, h_ref, hn_ref):
    h = r_ref[...] + jnp.dot(x_ref[...], w_ref[...], preferred_element_type=F32)
    h_ref[...] = h
    hn_ref[...] = _rms(h, g_ref[...]).astype(BF16)


def _resid_proj(x, w, resid, g, name, tm=512):
    t, k = x.shape
    d = w.shape[1]
    tm = min(tm, t)
    row_blk = pl.BlockSpec((tm, d), lambda i: (i, 0))
    return pl.pallas_call(
        _resid_kernel,
        grid=(t // tm,),
        in_specs=[
            pl.BlockSpec((tm, k), lambda i: (i, 0)),
            pl.BlockSpec((k, d), lambda i: (0, 0)),
            row_blk,
            pl.BlockSpec((1, d), lambda i: (0, 0)),
        ],
        out_specs=[row_blk, row_blk],
        out_shape=[jax.ShapeDtypeStruct((t, d), F32), jax.ShapeDtypeStruct((t, d), BF16)],
        compiler_params=pltpu.CompilerParams(
            dimension_semantics=("arbitrary",), vmem_limit_bytes=VMEM_LIMIT),
        name=name,
    )(x, w, resid, g)


def _memkv_kernel(m_ref, g_ref, wk_ref, wv_ref, k_ref, v_ref):
    mn = _rms(m_ref[...], g_ref[...]).astype(BF16)
    k_ref[...] = jnp.dot(mn, wk_ref[...], preferred_element_type=F32).astype(BF16)
    v_ref[...] = jnp.dot(mn, wv_ref[...], preferred_element_type=F32).astype(BF16)


def _memkv(mem, g, wk, wv, tn=512):
    m, d = mem.shape
    return pl.pallas_call(
        _memkv_kernel,
        grid=(d // tn,),
        in_specs=[
            pl.BlockSpec((m, d), lambda c: (0, 0)),
            pl.BlockSpec((1, d), lambda c: (0, 0)),
            pl.BlockSpec((d, tn), lambda c: (0, c)),
            pl.BlockSpec((d, tn), lambda c: (0, c)),
        ],
        out_specs=[pl.BlockSpec((m, tn), lambda c: (0, c))] * 2,
        out_shape=[jax.ShapeDtypeStruct((m, d), BF16)] * 2,
        compiler_params=pltpu.CompilerParams(
            dimension_semantics=("arbitrary",), vmem_limit_bytes=VMEM_LIMIT),
        name="memkv",
    )(mem, g, wk, wv)


def _xattn_kernel(hq_ref, wq_ref, k_ref, v_ref, o_ref):
    q = jnp.dot(hq_ref[...], wq_ref[...], preferred_element_type=F32)
    hd = MEM_HEAD_DIM
    for h in range(MEM_HEADS):
        sl = slice(h * hd, (h + 1) * hd)
        s = _mm(q[:, sl].astype(BF16), k_ref[:, sl], "nt") * (hd ** -0.5)
        e = jnp.exp(s - jnp.max(s, axis=-1, keepdims=True))
        p = e / jnp.sum(e, axis=-1, keepdims=True)
        o_ref[:, sl] = jnp.dot(p.astype(BF16), v_ref[:, sl], preferred_element_type=F32).astype(BF16)


def _xattn(hq, wq, k, v, tm=512):
    t, d = hq.shape
    tm = min(tm, t)
    m = k.shape[0]
    return pl.pallas_call(
        _xattn_kernel,
        grid=(t // tm,),
        in_specs=[
            pl.BlockSpec((tm, d), lambda i: (i, 0)),
            pl.BlockSpec((d, d), lambda i: (0, 0)),
            pl.BlockSpec((m, d), lambda i: (0, 0)),
            pl.BlockSpec((m, d), lambda i: (0, 0)),
        ],
        out_specs=pl.BlockSpec((tm, d), lambda i: (i, 0)),
        out_shape=jax.ShapeDtypeStruct((t, d), BF16),
        compiler_params=pltpu.CompilerParams(
            dimension_semantics=("arbitrary",), vmem_limit_bytes=VMEM_LIMIT),
        name="xattn",
    )(hq, wq, k, v)


def _pad_cols(a, n):
    return jnp.pad(a, ((0, 0), (0, n - a.shape[1])))


def _pad_rows(a, n):
    return jnp.pad(a, ((0, n - a.shape[0]), (0, 0)))


def _rw_layout(a):
    o1 = 3 * RW_WIDTH
    o2 = o1 + RW_DECAY_LORA
    o3 = o2 + RW_AAA_LORA
    return jnp.concatenate([
        a[:, :o1],
        _pad_cols(a[:, o1:o2], LANES),
        _pad_cols(a[:, o2:o3], LANES),
        _pad_cols(a[:, o3:], 2 * LANES),
    ], axis=1)


def kernel(x, mem, ffn1_norm, ffn1_w_gate, ffn1_w_up, ffn1_w_down, mix_norm, w_in, rw_shift_mu, rw_w0, rw_w2, rw_a0, rw_a2, rw_g2, rw_k_k, rw_k_a, rw_r_k, rw_ln_w, rw_ln_b, gdn_conv_w, gdn_a_log, gdn_dt_bias, gdn_norm_w, w_lift, w_out, mem_q_norm, mem_kv_norm, w_mq, w_mk, w_mv, w_mo, ffn2_norm, ffn2_w_gate, ffn2_w_up, ffn2_w_down, final_norm):
    b_, t, d = x.shape
    assert b_ == 1 and d == D_MODEL and ffn1_norm.shape[0] == 1
    l = 0
    bf = lambda a: a.astype(BF16)
    row = lambda a: a.reshape(1, -1)

    h0 = x.reshape(t, d)
    h1, u = _ffn_emit(h0, row(ffn1_norm[l]), bf(ffn1_w_gate[l]), bf(ffn1_w_up[l]), bf(ffn1_w_down[l]),
                      row(mix_norm[l]))

    w_in_l = bf(w_in[l])
    w_rw = _rw_layout(w_in_l[:, :RW_COLS])
    w_gdn = _pad_cols(w_in_l[:, RW_COLS:RW_COLS + GDN_COLS], GDN_PAD_COLS)
    w_gate = w_in_l[:, RW_COLS + GDN_COLS:]
    wga = w_gate[:, :D_MODEL]
    wgb = w_gate[:, D_MODEL:]

    y_rw = _rwkv(
        u, w_rw, _rw_layout(row(rw_shift_mu[l])), row(rw_w0[l]), _pad_rows(rw_w2[l], LANES),
        row(rw_a0[l]), _pad_rows(rw_a2[l], LANES), _pad_rows(rw_g2[l], 2 * LANES),
        row(rw_k_k[l]), row(rw_k_a[l]), row(rw_r_k[l]), row(rw_ln_w[l]), row(rw_ln_b[l]))

    nh = GDN_V_HEADS
    alog_pad = jnp.pad(row(gdn_a_log[l]), ((0, 0), (nh, LANES - 2 * nh)))
    dtb_pad = jnp.pad(row(gdn_dt_bias[l]), ((0, 0), (nh, LANES - 2 * nh)))
    y_gdn = _gdn(u, w_gdn, gdn_conv_w[l], alog_pad, dtb_pad, row(gdn_norm_w[l]))

    merged = _merge(u, y_rw, y_gdn, wga, wgb, bf(w_lift[l, 0]), bf(w_lift[l, 1]))
    h2, hq = _resid_proj(merged, bf(w_out[l]), h1, row(mem_q_norm[l]), "out_proj")

    mk, mv = _memkv(mem.reshape(mem.shape[1], d), row(mem_kv_norm[l]), bf(w_mk[l]), bf(w_mv[l]))
    att = _xattn(hq, bf(w_mq[l]), mk, mv)
    h3, h3n = _resid_proj(att, bf(w_mo[l]), h2, row(ffn2_norm[l]), "attn_out")

    y = _ffn_final(h3, h3n, bf(ffn2_w_gate[l]), bf(ffn2_w_up[l]), bf(ffn2_w_down[l]), row(final_norm))
    return y.reshape(b_, t, d)
```

```python
import functools

import jax
import jax.numpy as jnp
from jax import lax
from jax.experimental import pallas as pl
from jax.experimental.pallas import tpu as pltpu

F32 = jnp.float32
BF16 = jnp.bfloat16

D_MODEL = 2048
D_FF = 5632
RMS_EPS = 1e-6

RW_HEADS = 16
RW_HEAD_DIM = 64
RW_WIDTH = 1024
RW_DECAY_LORA = 64
RW_AAA_LORA = 64
RW_GATE_LORA = 160
RW_GN_EPS = 64e-5
RW_COLS = 3 * RW_WIDTH + RW_DECAY_LORA + RW_AAA_LORA + RW_GATE_LORA
LANES = 128
RW_OFF_WL = 3 * RW_WIDTH
RW_OFF_AL = RW_OFF_WL + LANES
RW_OFF_GL = RW_OFF_AL + LANES
RW_PAD_COLS = RW_OFF_GL + 2 * LANES

GDN_QK_HEADS = 4
GDN_V_HEADS = 8
GDN_HEAD_DIM = 128
GDN_QK_WIDTH = 512
GDN_V_WIDTH = 1024
GDN_CONV = 4
GDN_CONV_CH = 2 * GDN_QK_WIDTH + GDN_V_WIDTH
GDN_COLS = GDN_CONV_CH + GDN_V_WIDTH + 2 * GDN_V_HEADS
GDN_OFF_Z = GDN_CONV_CH
GDN_OFF_BA = GDN_CONV_CH + GDN_V_WIDTH
GDN_PAD_COLS = GDN_OFF_BA + LANES

MEM_HEADS = 4
MEM_HEAD_DIM = D_MODEL // MEM_HEADS

CHUNK = 64
SUBLANES = 8
VMEM_LIMIT = 56 * 1024 * 1024

_DIMS = {
    "nn": (((1,), (0,)), ((), ())),
    "nt": (((1,), (1,)), ((), ())),
    "tn": (((0,), (0,)), ((), ())),
}


def _split_bf16(x, n):
    if x.dtype == BF16:
        return [x]
    parts = []
    r = x
    for i in range(n):
        p = r.astype(BF16)
        parts.append(p)
        if i + 1 < n:
            r = r - p.astype(F32)
    return parts


def _mm(a, b, dims="nn", pa=1, pb=1):
    ap = _split_bf16(a, pa)
    bp = _split_bf16(b, pb)
    n = max(len(ap), len(bp))
    acc = None
    for i, ai in enumerate(ap):
        for j, bj in enumerate(bp):
            if i + j < n:
                t = lax.dot_general(ai, bj, _DIMS[dims], preferred_element_type=F32)
                acc = t if acc is None else acc + t
    return acc


def _rms(x, g, eps=RMS_EPS):
    ms = jnp.mean(x * x, axis=-1, keepdims=True)
    return x * lax.rsqrt(ms + eps) * g


def _sigmoid(x):
    return 1.0 / (1.0 + jnp.exp(-x))


def _softplus(x):
    return jnp.maximum(x, 0.0) + jnp.log(1.0 + jnp.exp(-jnp.abs(x)))


def _iota(shape, axis):
    return lax.broadcasted_iota(jnp.int32, shape, axis)


def _ones_where(cond):
    return jnp.where(cond, 1.0, 0.0).astype(BF16)


def _dot(a, b, dims="nn"):
    return lax.dot_general(a.astype(BF16), b.astype(BF16), _DIMS[dims], preferred_element_type=F32)


def _group_sum(x, group):
    shift = group.bit_length() - 1
    wide = 2 * LANES
    gi = lax.shift_right_logical(_iota((wide, wide), 0), shift)
    gj = lax.shift_right_logical(_iota((wide, wide), 1), shift)
    bd = _ones_where(gi == gj)
    xb = x.astype(BF16)
    outs = [jnp.dot(xb[:, c * wide:(c + 1) * wide], bd, preferred_element_type=F32)
            for c in range(x.shape[-1] // wide)]
    return outs[0] if len(outs) == 1 else jnp.concatenate(outs, axis=-1)


def _chunk_prefix_mask(tb):
    shift = CHUNK.bit_length() - 1
    i = _iota((tb, tb), 0)
    j = _iota((tb, tb), 1)
    same = lax.shift_right_logical(i, shift) == lax.shift_right_logical(j, shift)
    return _ones_where(same & (i >= j))


def _shift_rows(carry, x, j):
    xx = jnp.concatenate([carry, x], axis=0)
    return pltpu.roll(xx, j, 0)[SUBLANES:]


def _unit_lower_inverses(mats, out):
    n = mats[0].shape[0]
    eye = jnp.where(_iota((n, n), 0) == _iota((n, n), 1), 1.0, 0.0)
    invs = [eye + a for a in mats]
    pws = [a.astype(BF16) for a in mats]
    for _ in range(CHUNK.bit_length() - 2):
        pws = [_dot(pw, pw).astype(BF16) for pw in pws]
        yield
        invs = [inv + _dot(inv, pw) for inv, pw in zip(invs, pws)]
        yield
    out.extend(invs)


def _run_stages(*gens):
    live = list(gens)
    while live:
        for g in list(live):
            try:
                next(g)
            except StopIteration:
                live.remove(g)


def _chain_stages(gens):
    for g in gens:
        yield from g


def _swiglu_down(xn, wg_ref, wu_ref, wd_ref):
    gate = jnp.dot(xn, wg_ref[...], preferred_element_type=F32)
    up = jnp.dot(xn, wu_ref[...], preferred_element_type=F32)
    mid = (gate * _sigmoid(gate) * up).astype(BF16)
    return jnp.dot(mid, wd_ref[...], preferred_element_type=F32)


def _ffn_emit_kernel(x_ref, g_ref, wg_ref, wu_ref, wd_ref, g2_ref, h_ref, u_ref, xn_ref, acc_ref, *, nj):
    j = pl.program_id(1)

    @pl.when(j == 0)
    def _():
        xn_ref[...] = _rms(x_ref[...], g_ref[...]).astype(BF16)
        acc_ref[...] = jnp.zeros_like(acc_ref)

    acc_ref[...] += _swiglu_down(xn_ref[...], wg_ref, wu_ref, wd_ref)

    @pl.when(j == nj - 1)
    def _():
        h = x_ref[...] + 0.5 * acc_ref[...]
        h_ref[...] = h
        u_ref[...] = _rms(h, g2_ref[...]).astype(BF16)


def _ffn_emit(x, g, wg, wu, wd, g2, tm=512, tf=512):
    t, d = x.shape
    nj = wg.shape[1] // tf
    row_blk = pl.BlockSpec((tm, d), lambda i, j: (i, 0))
    vec = pl.BlockSpec((1, d), lambda i, j: (0, 0))
    return pl.pallas_call(
        functools.partial(_ffn_emit_kernel, nj=nj),
        grid=(t // tm, nj),
        in_specs=[row_blk, vec,
                  pl.BlockSpec((d, tf), lambda i, j: (0, j)),
                  pl.BlockSpec((d, tf), lambda i, j: (0, j)),
                  pl.BlockSpec((tf, d), lambda i, j: (j, 0)),
                  vec],
        out_specs=[row_blk, row_blk],
        out_shape=[jax.ShapeDtypeStruct((t, d), F32), jax.ShapeDtypeStruct((t, d), BF16)],
        scratch_shapes=[pltpu.VMEM((tm, d), BF16), pltpu.VMEM((tm, d), F32)],
        compiler_params=pltpu.CompilerParams(
            dimension_semantics=("arbitrary", "arbitrary"), vmem_limit_bytes=VMEM_LIMIT),
        name="ffn_emit",
    )(x, g, wg, wu, wd, g2)


def _ffn_final_kernel(x_ref, xn_ref, wg_ref, wu_ref, wd_ref, g2_ref, o_ref, *, nj):
    j = pl.program_id(1)

    @pl.when(j == 0)
    def _():
        o_ref[...] = jnp.zeros_like(o_ref)

    o_ref[...] += _swiglu_down(xn_ref[...], wg_ref, wu_ref, wd_ref)

    @pl.when(j == nj - 1)
    def _():
        o_ref[...] = _rms(x_ref[...] + 0.5 * o_ref[...], g2_ref[...])


def _ffn_final(x, xn, wg, wu, wd, g2, tm=1024, tf=256):
    t, d = x.shape
    tm = min(tm, t)
    nj = wg.shape[1] // tf
    row_blk = pl.BlockSpec((tm, d), lambda i, j: (i, 0))
    return pl.pallas_call(
        functools.partial(_ffn_final_kernel, nj=nj),
        grid=(t // tm, nj),
        in_specs=[row_blk, row_blk,
                  pl.BlockSpec((d, tf), lambda i, j: (0, j)),
                  pl.BlockSpec((d, tf), lambda i, j: (0, j)),
                  pl.BlockSpec((tf, d), lambda i, j: (j, 0)),
                  pl.BlockSpec((1, d), lambda i, j: (0, 0))],
        out_specs=row_blk,
        out_shape=jax.ShapeDtypeStruct((t, d), F32),
        compiler_params=pltpu.CompilerParams(
            dimension_semantics=("arbitrary", "arbitrary"), vmem_limit_bytes=VMEM_LIMIT),
        name="ffn_final",
    )(x, xn, wg, wu, wd, g2)


def _rwkv_kernel(u_ref, w_ref, mu_ref, w0_ref, w2_ref, a0_ref, a2_ref, g2_ref, kk_ref, ka_ref, rk_ref,
                 lnw_ref, lnb_ref, o_ref,
                 carry_ref, st_ref, at_s, rt_s, v_s, bkT_s, kbT_s, y_s, bonus_s, g_s,
                 *, tb):
    @pl.when(pl.program_id(0) == 0)
    def _():
        carry_ref[...] = jnp.zeros_like(carry_ref)
        st_ref[...] = jnp.zeros_like(st_ref)

    u = u_ref[...]

    def mixed(c0, c1):
        cols = jnp.dot(u, w_ref[:, c0:c1], preferred_element_type=F32)
        prev = _shift_rows(carry_ref[:, c0:c1], cols, 1)
        carry_ref[:, c0:c1] = cols[tb - SUBLANES:]
        return cols + (prev - cols) * mu_ref[:, c0:c1]

    xl = mixed(RW_OFF_WL, RW_PAD_COLS)
    wl = xl[:, :LANES]
    al = xl[:, LANES:2 * LANES]
    gl = xl[:, 2 * LANES:]
    zw = w0_ref[...] + _dot(jnp.tanh(wl), w2_ref[...])
    w_log = -_softplus(-zw) - 0.5
    lw = -jnp.exp(w_log)
    a = _sigmoid(a0_ref[...] + _dot(al, a2_ref[...]))
    g_s[...] = _dot(_sigmoid(gl), g2_ref[...])
    c = _mm(_chunk_prefix_mask(tb), lw, "nn", 1, 2)
    einv = jnp.exp(-c)

    k = mixed(RW_WIDTH, 2 * RW_WIDTH)
    kkv = k * kk_ref[...]
    kkn = kkv * lax.rsqrt(_group_sum(kkv * kkv, RW_HEAD_DIM) + 1e-6)
    k2 = k * (1.0 + (a - 1.0) * ka_ref[...])
    at_s[...] = (-kkn * jnp.exp(c - lw)).astype(BF16)
    L = CHUNK
    P2 = 2 * L
    n_chunks = tb // L
    bt = kkn * a * einv
    kt = k2 * einv
    for ci in range(n_chunks):
        rs = slice(ci * L, (ci + 1) * L)
        bkT_s[:, ci * P2:(ci + 1) * P2] = jnp.concatenate([bt[rs], kt[rs]], 0).T.astype(BF16)
        kbT_s[:, ci * P2:(ci + 1) * P2] = jnp.concatenate([kt[rs], bt[rs]], 0).T.astype(BF16)
    c_end = jnp.concatenate([c[(ci + 1) * L - 1:(ci + 1) * L] for ci in range(n_chunks)]
                            + [jnp.zeros((P2 - n_chunks, RW_WIDTH), F32)], 0)
    w_end_t = jnp.exp(c_end.T)

    r = mixed(0, RW_WIDTH)
    rt_s[...] = (r * jnp.exp(c)).astype(BF16)
    rkk = _group_sum(r * k2 * rk_ref[...], RW_HEAD_DIM)

    v = mixed(2 * RW_WIDTH, 3 * RW_WIDTH)
    v_s[...] = v.astype(BF16)
    bonus_s[...] = rkk * v

    n_pairs = RW_HEADS // 2
    lane_l = _iota((L, P2), 1)
    row_l = _iota((L, P2), 0)
    h0_l = lane_l < L
    strict_l = row_l > (lane_l & (L - 1))
    incl_l = row_l >= (lane_l & (L - 1))
    rr = _iota((P2, P2), 0)
    cc = _iota((P2, P2), 1)
    same_blk = ((rr ^ cc) & L) == 0
    strict_bd = same_blk & ((rr & (L - 1)) > (cc & (L - 1)))
    top = rr < L

    def only0(z):
        return jnp.where(h0_l, z, jnp.zeros_like(z))

    def only1(z):
        return jnp.where(h0_l, jnp.zeros_like(z), z)

    n_chunks = tb // L
    sls = [slice(p * P2, (p + 1) * P2) for p in range(n_pairs)]

    def ld(ref, it):
        return ref[it[0] * L:(it[0] + 1) * L, sls[it[1]]]

    def ld_t(ref, it):
        return ref[sls[it[1]], it[0] * P2:(it[0] + 1) * P2]

    def state_free(chunks, out):
        its = [(ci, p) for ci in chunks for p in range(n_pairs)]
        g0 = [_dot(jnp.concatenate([only0(ld(at_s, it)), only0(ld(rt_s, it))], 0), ld_t(bkT_s, it))
              for it in its]
        yield
        g1 = [_dot(jnp.concatenate([only1(ld(rt_s, it)), only1(ld(at_s, it))], 0), ld_t(kbT_s, it))
              for it in its]
        yield
        inv = []
        yield from _unit_lower_inverses(
            [jnp.where(strict_bd, jnp.where(top, a0_, a1_), 0.0) for a0_, a1_ in zip(g0, g1)], inv)
        akv = [_dot(jnp.where(strict_l, jnp.where(h0_l, a1_[L:], a0_[:L]), 0.0),
                    jnp.concatenate([only1(ld(v_s, it)), only0(ld(v_s, it))], 0))
               for it, a0_, a1_ in zip(its, g0, g1)]
        yield
        for k, it in enumerate(its):
            ly = jnp.concatenate([jnp.where(incl_l, g0[k][L:], 0.0), jnp.where(incl_l, g1[k][:L], 0.0)], 1)
            out[it] = (inv[k].astype(BF16), akv[k], ly.astype(BF16))

    state = [st_ref[p] for p in range(n_pairs)]

    def state_tail(ci, pre):
        rows = slice(ci * L, (ci + 1) * L)
        its = [(ci, p) for p in range(n_pairs)]
        xs = [_dot(jnp.concatenate([ld(at_s, it), ld(rt_s, it)], 0), state[p])
              for p, it in enumerate(its)]
        yield
        rhs = [xs[p][:L] + pre[it][1] for p, it in enumerate(its)]
        pp = [_dot(pre[it][0], jnp.concatenate([only0(rhs[p]), only1(rhs[p])], 0)) for p, it in enumerate(its)]
        yield
        pm = [(z[:L] + z[L:]).astype(BF16) for z in pp]
        for p, it in enumerate(its):
            vv = ld(v_s, it)
            ry = jnp.concatenate([only0(pm[p]), only0(vv), only1(vv), only1(pm[p])], 0)
            y_s[rows, sls[p]] = xs[p][L:] + _dot(pre[it][2], ry)
        yield
        for p, it in enumerate(its):
            upd = _dot(ld_t(bkT_s, it), jnp.concatenate([pm[p], ld(v_s, it)], 0))
            state[p] = (state[p] + jnp.where(same_blk, upd, 0.0)) * w_end_t[sls[p], ci:ci + 1]
        yield

    pre = {}
    first = list(range(max(1, n_chunks - 1)))
    rest = list(range(len(first), n_chunks))
    _run_stages(state_free(first, pre))
    _run_stages(state_free(rest, pre) if rest else iter(()),
                _chain_stages(state_tail(ci, pre) for ci in first))
    _run_stages(_chain_stages(state_tail(ci, pre) for ci in rest))
    for p in range(n_pairs):
        st_ref[p] = state[p]

    y = y_s[...]
    inv_n = 1.0 / RW_HEAD_DIM
    mu = _group_sum(y, RW_HEAD_DIM) * inv_n
    yc = y - mu
    var = _group_sum(yc * yc, RW_HEAD_DIM) * inv_n
    yn = yc * lax.rsqrt(var + RW_GN_EPS) * lnw_ref[...] + lnb_ref[...]
    o_ref[...] = ((yn + bonus_s[...]) * g_s[...]).astype(o_ref.dtype)


def _resident(a):
    return pl.BlockSpec(a.shape, lambda i: (0,) * a.ndim, pipeline_mode=pl.Buffered(1))


def _rwkv(u, w_rw, mu, w0, w2, a0, a2, g2, k_k, k_a, r_k, ln_w, ln_b, tb=256):
    t, d = u.shape
    tb = min(tb, t)
    w = RW_WIDTH

    def full(a):
        return pl.BlockSpec(a.shape, lambda i: (0,) * a.ndim)

    params = (mu, w0, w2, a0, a2, g2, k_k, k_a, r_k, ln_w, ln_b)
    return pl.pallas_call(
        functools.partial(_rwkv_kernel, tb=tb),
        grid=(t // tb,),
        in_specs=[pl.BlockSpec((tb, d), lambda i: (i, 0)), _resident(w_rw)] + [full(a) for a in params],
        out_specs=pl.BlockSpec((tb, w), lambda i: (i, 0)),
        out_shape=jax.ShapeDtypeStruct((t, w), BF16),
        scratch_shapes=[pltpu.VMEM((SUBLANES, RW_PAD_COLS), F32),
                        pltpu.VMEM((RW_HEADS // 2, 2 * CHUNK, 2 * CHUNK), F32)]
        + [pltpu.VMEM((tb, w), BF16)] * 3 + [pltpu.VMEM((w, 2 * tb), BF16)] * 2
        + [pltpu.VMEM((tb, w), F32)] * 3,
        compiler_params=pltpu.CompilerParams(
            dimension_semantics=("arbitrary",), vmem_limit_bytes=VMEM_LIMIT),
        name="rwkv7",
    )(u, w_rw, *params)


def _gdn_kernel(u_ref, w_ref, cw_ref, alog_ref, dtb_ref, nw_ref, o_ref,
                carry_ref, st_ref, q_s, k_s, vb_s, kbe_s, qe_s, kh_s, o_s, *, tb):
    @pl.when(pl.program_id(0) == 0)
    def _():
        carry_ref[...] = jnp.zeros_like(carry_ref)
        st_ref[...] = jnp.zeros_like(st_ref)

    u = u_ref[...]
    cw = cw_ref[...]

    def conv_silu(c0, c1):
        xin = jnp.dot(u, w_ref[:, c0:c1], preferred_element_type=F32)
        carry = carry_ref[:, c0:c1]
        acc = xin * cw[GDN_CONV - 1:GDN_CONV, c0:c1]
        for j in range(1, GDN_CONV):
            acc = acc + _shift_rows(carry, xin, j) * cw[GDN_CONV - 1 - j:GDN_CONV - j, c0:c1]
        carry_ref[:, c0:c1] = xin[tb - SUBLANES:]
        return acc * _sigmoid(acc)

    def l2n(z):
        return z * lax.rsqrt(jnp.sum(z * z, axis=-1, keepdims=True) + 1e-6)

    hd = GDN_HEAD_DIM
    ba = jnp.dot(u, w_ref[:, GDN_OFF_BA:GDN_PAD_COLS], preferred_element_type=F32)
    beta = _sigmoid(ba)
    gstep = -jnp.exp(alog_ref[...]) * _softplus(ba + dtb_ref[...])
    gc = _mm(_chunk_prefix_mask(tb), gstep, "nn", 1, 2)
    hh = _iota((LANES, GDN_V_WIDTH), 0)
    blk = lax.shift_right_logical(_iota((LANES, GDN_V_WIDTH), 1), hd.bit_length() - 1)
    e_beta = _ones_where(hh == blk)
    e_g = _ones_where(hh == blk + GDN_V_HEADS)
    bx = _dot(beta, e_beta)
    gcx = _mm(gc, e_g, "nn", 2, 1)
    egc = jnp.exp(gcx)
    L = CHUNK
    rep = GDN_V_HEADS // GDN_QK_HEADS
    hsl = [slice(h * hd, (h + 1) * hd) for h in range(GDN_V_HEADS)]

    kc = conv_silu(GDN_QK_WIDTH, 2 * GDN_QK_WIDTH)
    for h in range(GDN_V_HEADS):
        sl = hsl[h]
        if h % rep == 0:
            kn = l2n(kc[:, hsl[h // rep]])
            k_s[:, hsl[h // rep]] = kn.astype(BF16)
        kbe_s[:, sl] = (kn * bx[:, sl] * egc[:, sl]).astype(BF16)
        for ci in range(tb // L):
            rs = slice(ci * L, (ci + 1) * L)
            to_end = jnp.exp(gcx[(ci + 1) * L - 1:(ci + 1) * L, sl] - gcx[rs, sl])
            kh_s[rs, sl] = (kn[rs] * to_end).astype(BF16)

    qc = conv_silu(0, GDN_QK_WIDTH)
    for h in range(GDN_V_HEADS):
        if h % rep == 0:
            qn = l2n(qc[:, hsl[h // rep]]) * (hd ** -0.5)
            q_s[:, hsl[h // rep]] = qn.astype(BF16)
        qe_s[:, hsl[h]] = (qn * egc[:, hsl[h]]).astype(BF16)

    vb_s[...] = (conv_silu(2 * GDN_QK_WIDTH, GDN_CONV_CH) * bx).astype(BF16)

    P2 = 2 * L
    rr = _iota((P2, P2), 0)
    cc = _iota((P2, P2), 1)
    same_blk = ((rr ^ cc) & L) == 0
    lr = rr & (L - 1)
    lc = cc & (L - 1)
    strict_bd = same_blk & (lr > lc)
    incl_bd = same_blk & (lr >= lc)
    n_pairs = GDN_QK_HEADS

    n_chunks = tb // L

    def state_free(chunks, out):
        its = [(ci, p) for ci in chunks for p in range(n_pairs)]
        gm = []
        for ci, p in its:
            rows = slice(ci * L, (ci + 1) * L)
            kp = k_s[rows, hsl[p]]
            gm.append(_dot(jnp.concatenate([kp, q_s[rows, hsl[p]]], 0),
                           jnp.concatenate([kp, kp], 0), "nt"))
        yield
        a_bd = []
        a_qk = []
        for (ci, p), g in zip(its, gm):
            rows = slice(ci * L, (ci + 1) * L)
            h0, h1 = hsl[2 * p], hsl[2 * p + 1]
            gcol = jnp.concatenate([gcx[rows, h0], gcx[rows, h1]], 0)
            diff = gcol - gcol.T
            decay = jnp.where(incl_bd, jnp.exp(jnp.where(incl_bd, diff, 0.0)), 0.0)
            bcol = jnp.concatenate([bx[rows, h0], bx[rows, h1]], 0)
            a_bd.append(jnp.where(strict_bd, -(jnp.concatenate([g[:L], g[:L]], 0) * bcol * decay), 0.0))
            a_qk.append((jnp.concatenate([g[L:], g[L:]], 0) * decay).astype(BF16))
        inv = []
        yield from _unit_lower_inverses(a_bd, inv)
        for k, (ci, p) in enumerate(its):
            rows = slice(ci * L, (ci + 1) * L)
            h0, h1 = hsl[2 * p], hsl[2 * p + 1]
            rhs = jnp.concatenate([jnp.concatenate([vb_s[rows, h0], kbe_s[rows, h0]], 1),
                                   jnp.concatenate([vb_s[rows, h1], kbe_s[rows, h1]], 1)], 0)
            out[(ci, p)] = (_dot(inv[k], rhs), a_qk[k])
        yield

    state = [st_ref[h] for h in range(GDN_V_HEADS)]

    def state_tail(ci, pre):
        rows = slice(ci * L, (ci + 1) * L)

        def uw_of(h):
            return pre[(ci, h // 2)][0][(h % 2) * L:(h % 2 + 1) * L]

        x = [_dot(jnp.concatenate([uw_of(h)[:, hd:].astype(BF16), qe_s[rows, hsl[h]]], 0), state[h])
             for h in range(GDN_V_HEADS)]
        yield
        vnew = [(uw_of(h)[:, :hd] - x[h][:L]).astype(BF16) for h in range(GDN_V_HEADS)]
        for p in range(n_pairs):
            o_pair = _dot(pre[(ci, p)][1], jnp.concatenate([vnew[2 * p], vnew[2 * p + 1]], 0))
            o_s[rows, hsl[2 * p]] = x[2 * p][L:] + o_pair[:L]
            o_s[rows, hsl[2 * p + 1]] = x[2 * p + 1][L:] + o_pair[L:]
        yield
        for h in range(GDN_V_HEADS):
            state[h] = (state[h] * jnp.exp(gcx[(ci + 1) * L - 1:(ci + 1) * L, hsl[h]])
                        + _dot(kh_s[rows, hsl[h]], vnew[h], "tn"))
        yield

    pre = {}
    _run_stages(state_free(list(range(n_chunks)), pre))
    _run_stages(_chain_stages(state_tail(ci, pre) for ci in range(n_chunks)))
    for h in range(GDN_V_HEADS):
        st_ref[h] = state[h]

    nw = nw_ref[...]
    for h in range(GDN_V_HEADS):
        sl = hsl[h]
        zz = jnp.dot(u, w_ref[:, GDN_OFF_Z + h * hd:GDN_OFF_Z + (h + 1) * hd], preferred_element_type=F32)
        o = o_s[:, sl]
        on = o * lax.rsqrt(jnp.mean(o * o, axis=-1, keepdims=True) + RMS_EPS) * nw
        o_ref[:, sl] = (on * (zz * _sigmoid(zz))).astype(o_ref.dtype)


def _gdn(u, w_gdn, conv_w, alog_pad, dtb_pad, norm_w, tb=256):
    t, d = u.shape
    tb = min(tb, t)

    def full(a):
        return pl.BlockSpec(a.shape, lambda i: (0,) * a.ndim)

    params = (conv_w, alog_pad, dtb_pad, norm_w)
    qk = pltpu.VMEM((tb, GDN_QK_WIDTH), BF16)
    return pl.pallas_call(
        functools.partial(_gdn_kernel, tb=tb),
        grid=(t // tb,),
        in_specs=[pl.BlockSpec((tb, d), lambda i: (i, 0)), _resident(w_gdn)] + [full(a) for a in params],
        out_specs=pl.BlockSpec((tb, GDN_V_WIDTH), lambda i: (i, 0)),
        out_shape=jax.ShapeDtypeStruct((t, GDN_V_WIDTH), BF16),
        scratch_shapes=[pltpu.VMEM((SUBLANES, GDN_CONV_CH), F32),
                        pltpu.VMEM((GDN_V_HEADS, GDN_HEAD_DIM, GDN_HEAD_DIM), F32),
                        qk, qk]
        + [pltpu.VMEM((tb, GDN_V_WIDTH), BF16)] * 4 + [pltpu.VMEM((tb, GDN_V_WIDTH), F32)],
        compiler_params=pltpu.CompilerParams(
            dimension_semantics=("arbitrary",), vmem_limit_bytes=VMEM_LIMIT),
        name="gdn",
    )(u, w_gdn, *params)


def _merge_kernel(u_ref, ya_ref, yb_ref, wga_ref, wgb_ref, la_ref, lb_ref, o_ref):
    u = u_ref[...]
    ga = _sigmoid(jnp.dot(u, wga_ref[...], preferred_element_type=F32))
    gb = _sigmoid(jnp.dot(u, wgb_ref[...], preferred_element_type=F32))
    ma = jnp.dot(ya_ref[...], la_ref[...], preferred_element_type=F32)
    mb = jnp.dot(yb_ref[...], lb_ref[...], preferred_element_type=F32)
    o_ref[...] = (ga * ma + gb * mb).astype(o_ref.dtype)


def _merge(u, ya, yb, wga, wgb, la, lb, tm=1024, tn=512):
    t, d = u.shape
    tm = min(tm, t)
    bw = ya.shape[1]
    return pl.pallas_call(
        _merge_kernel,
        grid=(d // tn, t // tm),
        in_specs=[
            pl.BlockSpec((tm, d), lambda c, i: (i, 0)),
            pl.BlockSpec((tm, bw), lambda c, i: (i, 0)),
            pl.BlockSpec((tm, bw), lambda c, i: (i, 0)),
            pl.BlockSpec((d, tn), lambda c, i: (0, c)),
            pl.BlockSpec((d, tn), lambda c, i: (0, c)),
            pl.BlockSpec((bw, tn), lambda c, i: (0, c)),
            pl.BlockSpec((bw, tn), lambda c, i: (0, c)),
        ],
        out_specs=pl.BlockSpec((tm, tn), lambda c, i: (i, c)),
        out_shape=jax.ShapeDtypeStruct((t, d), BF16),
        compiler_params=pltpu.CompilerParams(
            dimension_semantics=("arbitrary", "arbitrary"), vmem_limit_bytes=VMEM_LIMIT),
        name="merge",
    )(u, ya, yb, wga, wgb, la, lb)


def _resid_kernel(x_ref, w_ref, r_ref, g_ref, h_ref, hn_ref):
    h = r_ref[...] + jnp.dot(x_ref[...], w_ref[...], preferred_element_type=F32)
    h_ref[...] = h
    hn_ref[...] = _rms(h, g_ref[...]).astype(BF16)


def _resid_proj(x, w, resid, g, name, tm=512):
    t, k = x.shape
    d = w.shape[1]
    tm = min(tm, t)
    row_blk = pl.BlockSpec((tm, d), lambda i: (i, 0))
    return pl.pallas_call(
        _resid_kernel,
        grid=(t // tm,),
        in_specs=[
            pl.BlockSpec((tm, k), lambda i: (i, 0)),
            pl.BlockSpec((k, d), lambda i: (0, 0)),
            row_blk,
            pl.BlockSpec((1, d), lambda i: (0, 0)),
        ],
        out_specs=[row_blk, row_blk],
        out_shape=[jax.ShapeDtypeStruct((t, d), F32), jax.ShapeDtypeStruct((t, d), BF16)],
        compiler_params=pltpu.CompilerParams(
            dimension_semantics=("arbitrary",), vmem_limit_bytes=VMEM_LIMIT),
        name=name,
    )(x, w, resid, g)


def _memkv_kernel(m_ref, g_ref, wk_ref, wv_ref, k_ref, v_ref):
    mn = _rms(m_ref[...], g_ref[...]).astype(BF16)
    k_ref[...] = jnp.dot(mn, wk_ref[...], preferred_element_type=F32).astype(BF16)
    v_ref[...] = jnp.dot(mn, wv_ref[...], preferred_element_type=F32).astype(BF16)


def _memkv(mem, g, wk, wv, tn=512):
    m, d = mem.shape
    return pl.pallas_call(
        _memkv_kernel,
        grid=(d // tn,),
        in_specs=[
            pl.BlockSpec((m, d), lambda c: (0, 0)),
            pl.BlockSpec((1, d), lambda c: (0, 0)),
            pl.BlockSpec((d, tn), lambda c: (0, c)),
            pl.BlockSpec((d, tn), lambda c: (0, c)),
        ],
        out_specs=[pl.BlockSpec((m, tn), lambda c: (0, c))] * 2,
        out_shape=[jax.ShapeDtypeStruct((m, d), BF16)] * 2,
        compiler_params=pltpu.CompilerParams(
            dimension_semantics=("arbitrary",), vmem_limit_bytes=VMEM_LIMIT),
        name="memkv",
    )(mem, g, wk, wv)


def _xattn_kernel(hq_ref, wq_ref, k_ref, v_ref, o_ref):
    q = jnp.dot(hq_ref[...], wq_ref[...], preferred_element_type=F32)
    hd = MEM_HEAD_DIM
    for h in range(MEM_HEADS):
        sl = slice(h * hd, (h + 1) * hd)
        s = _mm(q[:, sl].astype(BF16), k_ref[:, sl], "nt") * (hd ** -0.5)
        e = jnp.exp(s - jnp.max(s, axis=-1, keepdims=True))
        p = e / jnp.sum(e, axis=-1, keepdims=True)
        o_ref[:, sl] = jnp.dot(p.astype(BF16), v_ref[:, sl], preferred_element_type=F32).astype(BF16)


def _xattn(hq, wq, k, v, tm=512):
    t, d = hq.shape
    tm = min(tm, t)
    m = k.shape[0]
    return pl.pallas_call(
        _xattn_kernel,
        grid=(t // tm,),
        in_specs=[
            pl.BlockSpec((tm, d), lambda i: (i, 0)),
            pl.BlockSpec((d, d), lambda i: (0, 0)),
            pl.BlockSpec((m, d), lambda i: (0, 0)),
            pl.BlockSpec((m, d), lambda i: (0, 0)),
        ],
        out_specs=pl.BlockSpec((tm, d), lambda i: (i, 0)),
        out_shape=jax.ShapeDtypeStruct((t, d), BF16),
        compiler_params=pltpu.CompilerParams(
            dimension_semantics=("arbitrary",), vmem_limit_bytes=VMEM_LIMIT),
        name="xattn",
    )(hq, wq, k, v)


def _pad_cols(a, n):
    return jnp.pad(a, ((0, 0), (0, n - a.shape[1])))


def _pad_rows(a, n):
    return jnp.pad(a, ((0, n - a.shape[0]), (0, 0)))


def _rw_layout(a):
    o1 = 3 * RW_WIDTH
    o2 = o1 + RW_DECAY_LORA
    o3 = o2 + RW_AAA_LORA
    return jnp.concatenate([
        a[:, :o1],
        _pad_cols(a[:, o1:o2], LANES),
        _pad_cols(a[:, o2:o3], LANES),
        _pad_cols(a[:, o3:], 2 * LANES),
    ], axis=1)


def kernel(x, mem, ffn1_norm, ffn1_w_gate, ffn1_w_up, ffn1_w_down, mix_norm, w_in, rw_shift_mu, rw_w0, rw_w2, rw_a0, rw_a2, rw_g2, rw_k_k, rw_k_a, rw_r_k, rw_ln_w, rw_ln_b, gdn_conv_w, gdn_a_log, gdn_dt_bias, gdn_norm_w, w_lift, w_out, mem_q_norm, mem_kv_norm, w_mq, w_mk, w_mv, w_mo, ffn2_norm, ffn2_w_gate, ffn2_w_up, ffn2_w_down, final_norm):
    b_, t, d = x.shape
    assert b_ == 1 and d == D_MODEL and ffn1_norm.shape[0] == 1
    l = 0
    bf = lambda a: a.astype(BF16)
    row = lambda a: a.reshape(1, -1)

    h0 = x.reshape(t, d)
    h1, u = _ffn_emit(h0, row(ffn1_norm[l]), bf(ffn1_w_gate[l]), bf(ffn1_w_up[l]), bf(ffn1_w_down[l]),
                      row(mix_norm[l]))

    w_in_l = bf(w_in[l])
    w_rw = _rw_layout(w_in_l[:, :RW_COLS])
    w_gdn = _pad_cols(w_in_l[:, RW_COLS:RW_COLS + GDN_COLS], GDN_PAD_COLS)
    w_gate = w_in_l[:, RW_COLS + GDN_COLS:]
    wga = w_gate[:, :D_MODEL]
    wgb = w_gate[:, D_MODEL:]

    y_rw = _rwkv(
        u, w_rw, _rw_layout(row(rw_shift_mu[l])), row(rw_w0[l]), _pad_rows(rw_w2[l], LANES),
        row(rw_a0[l]), _pad_rows(rw_a2[l], LANES), _pad_rows(rw_g2[l], 2 * LANES),
        row(rw_k_k[l]), row(rw_k_a[l]), row(rw_r_k[l]), row(rw_ln_w[l]), row(rw_ln_b[l]))

    nh = GDN_V_HEADS
    alog_pad = jnp.pad(row(gdn_a_log[l]), ((0, 0), (nh, LANES - 2 * nh)))
    dtb_pad = jnp.pad(row(gdn_dt_bias[l]), ((0, 0), (nh, LANES - 2 * nh)))
    y_gdn = _gdn(u, w_gdn, gdn_conv_w[l], alog_pad, dtb_pad, row(gdn_norm_w[l]))

    merged = _merge(u, y_rw, y_gdn, wga, wgb, bf(w_lift[l, 0]), bf(w_lift[l, 1]))
    h2, hq = _resid_proj(merged, bf(w_out[l]), h1, row(mem_q_norm[l]), "out_proj")

    mk, mv = _memkv(mem.reshape(mem.shape[1], d), row(mem_kv_norm[l]), bf(w_mk[l]), bf(w_mv[l]))
    att = _xattn(hq, bf(w_mq[l]), mk, mv)
    h3, h3n = _resid_proj(att, bf(w_mo[l]), h2, row(ffn2_norm[l]), "attn_out")

    y = _ffn_final(h3, h3n, bf(ffn2_w_gate[l]), bf(ffn2_w_up[l]), bf(ffn2_w_down[l]), row(final_norm))
    return y.reshape(b_, t, d)
```

```python
import functools

import jax
import jax.numpy as jnp
from jax import lax
from jax.experimental import pallas as pl
from jax.experimental.pallas import tpu as pltpu

F32 = jnp.float32
BF16 = jnp.bfloat16

D_MODEL = 2048
D_FF = 5632
RMS_EPS = 1e-6

RW_HEADS = 16
RW_HEAD_DIM = 64
RW_WIDTH = 1024
RW_DECAY_LORA = 64
RW_AAA_LORA = 64
RW_GATE_LORA = 160
RW_GN_EPS = 64e-5
RW_COLS = 3 * RW_WIDTH + RW_DECAY_LORA + RW_AAA_LORA + RW_GATE_LORA
LANES = 128
RW_OFF_WL = 3 * RW_WIDTH
RW_OFF_AL = RW_OFF_WL + LANES
RW_OFF_GL = RW_OFF_AL + LANES
RW_PAD_COLS = RW_OFF_GL + 2 * LANES

GDN_QK_HEADS = 4
GDN_V_HEADS = 8
GDN_HEAD_DIM = 128
GDN_QK_WIDTH = 512
GDN_V_WIDTH = 1024
GDN_CONV = 4
GDN_CONV_CH = 2 * GDN_QK_WIDTH + GDN_V_WIDTH
GDN_COLS = GDN_CONV_CH + GDN_V_WIDTH + 2 * GDN_V_HEADS
GDN_OFF_Z = GDN_CONV_CH
GDN_OFF_BA = GDN_CONV_CH + GDN_V_WIDTH
GDN_PAD_COLS = GDN_OFF_BA + LANES

MEM_HEADS = 4
MEM_HEAD_DIM = D_MODEL // MEM_HEADS

CHUNK = 64
SUBLANES = 8
VMEM_LIMIT = 56 * 1024 * 1024

_DIMS = {
    "nn": (((1,), (0,)), ((), ())),
    "nt": (((1,), (1,)), ((), ())),
    "tn": (((0,), (0,)), ((), ())),
}


def _split_bf16(x, n):
    if x.dtype == BF16:
        return [x]
    parts = []
    r = x
    for i in range(n):
        p = r.astype(BF16)
        parts.append(p)
        if i + 1 < n:
            r = r - p.astype(F32)
    return parts


def _mm(a, b, dims="nn", pa=1, pb=1):
    ap = _split_bf16(a, pa)
    bp = _split_bf16(b, pb)
    n = max(len(ap), len(bp))
    acc = None
    for i, ai in enumerate(ap):
        for j, bj in enumerate(bp):
            if i + j < n:
                t = lax.dot_general(ai, bj, _DIMS[dims], preferred_element_type=F32)
                acc = t if acc is None else acc + t
    return acc


def _rms(x, g, eps=RMS_EPS):
    ms = jnp.mean(x * x, axis=-1, keepdims=True)
    return x * lax.rsqrt(ms + eps) * g


def _sigmoid(x):
    return 1.0 / (1.0 + jnp.exp(-x))


def _softplus(x):
    return jnp.maximum(x, 0.0) + jnp.log(1.0 + jnp.exp(-jnp.abs(x)))


def _iota(shape, axis):
    return lax.broadcasted_iota(jnp.int32, shape, axis)


def _ones_where(cond):
    return jnp.where(cond, 1.0, 0.0).astype(BF16)


def _dot(a, b, dims="nn"):
    return lax.dot_general(a.astype(BF16), b.astype(BF16), _DIMS[dims], preferred_element_type=F32)


def _group_sum(x, group):
    shift = group.bit_length() - 1
    wide = 2 * LANES
    gi = lax.shift_right_logical(_iota((wide, wide), 0), shift)
    gj = lax.shift_right_logical(_iota((wide, wide), 1), shift)
    bd = _ones_where(gi == gj)
    xb = x.astype(BF16)
    outs = [jnp.dot(xb[:, c * wide:(c + 1) * wide], bd, preferred_element_type=F32)
            for c in range(x.shape[-1] // wide)]
    return outs[0] if len(outs) == 1 else jnp.concatenate(outs, axis=-1)


def _chunk_prefix_mask(tb):
    shift = CHUNK.bit_length() - 1
    i = _iota((tb, tb), 0)
    j = _iota((tb, tb), 1)
    same = lax.shift_right_logical(i, shift) == lax.shift_right_logical(j, shift)
    return _ones_where(same & (i >= j))


def _shift_rows(carry, x, j):
    xx = jnp.concatenate([carry, x], axis=0)
    return pltpu.roll(xx, j, 0)[SUBLANES:]


def _unit_lower_inverses(mats, out):
    n = mats[0].shape[0]
    eye = jnp.where(_iota((n, n), 0) == _iota((n, n), 1), 1.0, 0.0)
    invs = [eye + a for a in mats]
    pws = [a.astype(BF16) for a in mats]
    for _ in range(CHUNK.bit_length() - 2):
        pws = [_dot(pw, pw).astype(BF16) for pw in pws]
        yield
        invs = [inv + _dot(inv, pw) for inv, pw in zip(invs, pws)]
        yield
    out.extend(invs)


def _run_stages(*gens):
    live = list(gens)
    while live:
        for g in list(live):
            try:
                next(g)
            except StopIteration:
                live.remove(g)


def _chain_stages(gens):
    for g in gens:
        yield from g


def _swiglu_down(xn, wg_ref, wu_ref, wd_ref):
    gate = jnp.dot(xn, wg_ref[...], preferred_element_type=F32)
    up = jnp.dot(xn, wu_ref[...], preferred_element_type=F32)
    mid = (gate * _sigmoid(gate) * up).astype(BF16)
    return jnp.dot(mid, wd_ref[...], preferred_element_type=F32)


def _ffn_emit_kernel(x_ref, g_ref, wg_ref, wu_ref, wd_ref, g2_ref, h_ref, u_ref, xn_ref, acc_ref, *, nj):
    j = pl.program_id(1)

    @pl.when(j == 0)
    def _():
        xn_ref[...] = _rms(x_ref[...], g_ref[...]).astype(BF16)
        acc_ref[...] = jnp.zeros_like(acc_ref)

    acc_ref[...] += _swiglu_down(xn_ref[...], wg_ref, wu_ref, wd_ref)

    @pl.when(j == nj - 1)
    def _():
        h = x_ref[...] + 0.5 * acc_ref[...]
        h_ref[...] = h
        u_ref[...] = _rms(h, g2_ref[...]).astype(BF16)


def _ffn_emit(x, g, wg, wu, wd, g2, tm=512, tf=512):
    t, d = x.shape
    nj = wg.shape[1] // tf
    row_blk = pl.BlockSpec((tm, d), lambda i, j: (i, 0))
    vec = pl.BlockSpec((1, d), lambda i, j: (0, 0))
    return pl.pallas_call(
        functools.partial(_ffn_emit_kernel, nj=nj),
        grid=(t // tm, nj),
        in_specs=[row_blk, vec,
                  pl.BlockSpec((d, tf), lambda i, j: (0, j)),
                  pl.BlockSpec((d, tf), lambda i, j: (0, j)),
                  pl.BlockSpec((tf, d), lambda i, j: (j, 0)),
                  vec],
        out_specs=[row_blk, row_blk],
        out_shape=[jax.ShapeDtypeStruct((t, d), F32), jax.ShapeDtypeStruct((t, d), BF16)],
        scratch_shapes=[pltpu.VMEM((tm, d), BF16), pltpu.VMEM((tm, d), F32)],
        compiler_params=pltpu.CompilerParams(
            dimension_semantics=("arbitrary", "arbitrary"), vmem_limit_bytes=VMEM_LIMIT),
        name="ffn_emit",
    )(x, g, wg, wu, wd, g2)


def _ffn_final_kernel(x_ref, xn_ref, wg_ref, wu_ref, wd_ref, g2_ref, o_ref, *, nj):
    j = pl.program_id(1)

    @pl.when(j == 0)
    def _():
        o_ref[...] = jnp.zeros_like(o_ref)

    o_ref[...] += _swiglu_down(xn_ref[...], wg_ref, wu_ref, wd_ref)

    @pl.when(j == nj - 1)
    def _():
        o_ref[...] = _rms(x_ref[...] + 0.5 * o_ref[...], g2_ref[...])


def _ffn_final(x, xn, wg, wu, wd, g2, tm=1024, tf=256):
    t, d = x.shape
    tm = min(tm, t)
    nj = wg.shape[1] // tf
    row_blk = pl.BlockSpec((tm, d), lambda i, j: (i, 0))
    return pl.pallas_call(
        functools.partial(_ffn_final_kernel, nj=nj),
        grid=(t // tm, nj),
        in_specs=[row_blk, row_blk,
                  pl.BlockSpec((d, tf), lambda i, j: (0, j)),
                  pl.BlockSpec((d, tf), lambda i, j: (0, j)),
                  pl.BlockSpec((tf, d), lambda i, j: (j, 0)),
                  pl.BlockSpec((1, d), lambda i, j: (0, 0))],
        out_specs=row_blk,
        out_shape=jax.ShapeDtypeStruct((t, d), F32),
        compiler_params=pltpu.CompilerParams(
            dimension_semantics=("arbitrary", "arbitrary"), vmem_limit_bytes=VMEM_LIMIT),
        name="ffn_final",
    )(x, xn, wg, wu, wd, g2)


def _rwkv_kernel(u_ref, w_ref, mu_ref, w0_ref, w2_ref, a0_ref, a2_ref, g2_ref, kk_ref, ka_ref, rk_ref,
                 lnw_ref, lnb_ref, o_ref,
                 carry_ref, st_ref, at_s, rt_s, v_s, bkT_s, kbT_s, y_s, bonus_s, g_s,
                 *, tb):
    @pl.when(pl.program_id(0) == 0)
    def _():
        carry_ref[...] = jnp.zeros_like(carry_ref)
        st_ref[...] = jnp.zeros_like(st_ref)

    u = u_ref[...]

    def mixed(c0, c1):
        cols = jnp.dot(u, w_ref[:, c0:c1], preferred_element_type=F32)
        prev = _shift_rows(carry_ref[:, c0:c1], cols, 1)
        carry_ref[:, c0:c1] = cols[tb - SUBLANES:]
        return cols + (prev - cols) * mu_ref[:, c0:c1]

    xl = mixed(RW_OFF_WL, RW_PAD_COLS)
    k = mixed(RW_WIDTH, 2 * RW_WIDTH)
    wl = xl[:, :LANES]
    al = xl[:, LANES:2 * LANES]
    gl = xl[:, 2 * LANES:]
    zw = w0_ref[...] + _dot(jnp.tanh(wl), w2_ref[...])
    w_log = -_softplus(-zw) - 0.5
    lw = -jnp.exp(w_log)
    a = _sigmoid(a0_ref[...] + _dot(al, a2_ref[...]))
    g_s[...] = _dot(_sigmoid(gl), g2_ref[...])
    c = _mm(_chunk_prefix_mask(tb), lw, "nn", 1, 2)
    einv = jnp.exp(-c)

    r = mixed(0, RW_WIDTH)
    kkv = k * kk_ref[...]
    kkn = kkv * lax.rsqrt(_group_sum(kkv * kkv, RW_HEAD_DIM) + 1e-6)
    k2 = k * (1.0 + (a - 1.0) * ka_ref[...])
    at_s[...] = (-kkn * jnp.exp(c - lw)).astype(BF16)
    L = CHUNK
    P2 = 2 * L
    n_chunks = tb // L
    bt = kkn * a * einv
    kt = k2 * einv
    for ci in range(n_chunks):
        rs = slice(ci * L, (ci + 1) * L)
        bkT_s[:, ci * P2:(ci + 1) * P2] = jnp.concatenate([bt[rs], kt[rs]], 0).T.astype(BF16)
        kbT_s[:, ci * P2:(ci + 1) * P2] = jnp.concatenate([kt[rs], bt[rs]], 0).T.astype(BF16)
    c_end = jnp.concatenate([c[(ci + 1) * L - 1:(ci + 1) * L] for ci in range(n_chunks)]
                            + [jnp.zeros((P2 - n_chunks, RW_WIDTH), F32)], 0)
    w_end_t = jnp.exp(c_end.T)

    v = mixed(2 * RW_WIDTH, 3 * RW_WIDTH)
    rt_s[...] = (r * jnp.exp(c)).astype(BF16)
    rkk = _group_sum(r * k2 * rk_ref[...], RW_HEAD_DIM)
    v_s[...] = v.astype(BF16)
    bonus_s[...] = rkk * v

    n_pairs = RW_HEADS // 2
    lane_l = _iota((L, P2), 1)
    row_l = _iota((L, P2), 0)
    h0_l = lane_l < L
    strict_l = row_l > (lane_l & (L - 1))
    incl_l = row_l >= (lane_l & (L - 1))
    rr = _iota((P2, P2), 0)
    cc = _iota((P2, P2), 1)
    same_blk = ((rr ^ cc) & L) == 0
    strict_bd = same_blk & ((rr & (L - 1)) > (cc & (L - 1)))
    top = rr < L

    def only0(z):
        return jnp.where(h0_l, z, jnp.zeros_like(z))

    def only1(z):
        return jnp.where(h0_l, jnp.zeros_like(z), z)

    n_chunks = tb // L
    sls = [slice(p * P2, (p + 1) * P2) for p in range(n_pairs)]

    def ld(ref, it):
        return ref[it[0] * L:(it[0] + 1) * L, sls[it[1]]]

    def ld_t(ref, it):
        return ref[sls[it[1]], it[0] * P2:(it[0] + 1) * P2]

    def state_free(chunks, out):
        its = [(ci, p) for ci in chunks for p in range(n_pairs)]
        g0 = [_dot(jnp.concatenate([only0(ld(at_s, it)), only0(ld(rt_s, it))], 0), ld_t(bkT_s, it))
              for it in its]
        yield
        g1 = [_dot(jnp.concatenate([only1(ld(rt_s, it)), only1(ld(at_s, it))], 0), ld_t(kbT_s, it))
              for it in its]
        yield
        inv = []
        yield from _unit_lower_inverses(
            [jnp.where(strict_bd, jnp.where(top, a0_, a1_), 0.0) for a0_, a1_ in zip(g0, g1)], inv)
        akv = [_dot(jnp.where(strict_l, jnp.where(h0_l, a1_[L:], a0_[:L]), 0.0),
                    jnp.concatenate([only1(ld(v_s, it)), only0(ld(v_s, it))], 0))
               for it, a0_, a1_ in zip(its, g0, g1)]
        yield
        for k, it in enumerate(its):
            ly = jnp.concatenate([jnp.where(incl_l, g0[k][L:], 0.0), jnp.where(incl_l, g1[k][:L], 0.0)], 1)
            out[it] = (inv[k].astype(BF16), akv[k], ly.astype(BF16))

    state = [st_ref[p] for p in range(n_pairs)]

    def state_tail(ci, pre):
        rows = slice(ci * L, (ci + 1) * L)
        its = [(ci, p) for p in range(n_pairs)]
        xs = [_dot(jnp.concatenate([ld(at_s, it), ld(rt_s, it)], 0), state[p])
              for p, it in enumerate(its)]
        yield
        rhs = [xs[p][:L] + pre[it][1] for p, it in enumerate(its)]
        pp = [_dot(pre[it][0], jnp.concatenate([only0(rhs[p]), only1(rhs[p])], 0)) for p, it in enumerate(its)]
        yield
        pm = [(z[:L] + z[L:]).astype(BF16) for z in pp]
        for p, it in enumerate(its):
            vv = ld(v_s, it)
            ry = jnp.concatenate([only0(pm[p]), only0(vv), only1(vv), only1(pm[p])], 0)
            y_s[rows, sls[p]] = xs[p][L:] + _dot(pre[it][2], ry)
        yield
        for p, it in enumerate(its):
            upd = _dot(ld_t(bkT_s, it), jnp.concatenate([pm[p], ld(v_s, it)], 0))
            state[p] = (state[p] + jnp.where(same_blk, upd, 0.0)) * w_end_t[sls[p], ci:ci + 1]
        yield

    pre = {}
    first = list(range(max(1, n_chunks - 1)))
    rest = list(range(len(first), n_chunks))
    _run_stages(state_free(first, pre))
    _run_stages(state_free(rest, pre) if rest else iter(()),
                _chain_stages(state_tail(ci, pre) for ci in first))
    _run_stages(_chain_stages(state_tail(ci, pre) for ci in rest))
    for p in range(n_pairs):
        st_ref[p] = state[p]

    y = y_s[...]
    inv_n = 1.0 / RW_HEAD_DIM
    mu = _group_sum(y, RW_HEAD_DIM) * inv_n
    yc = y - mu
    var = _group_sum(yc * yc, RW_HEAD_DIM) * inv_n
    yn = yc * lax.rsqrt(var + RW_GN_EPS) * lnw_ref[...] + lnb_ref[...]
    o_ref[...] = ((yn + bonus_s[...]) * g_s[...]).astype(o_ref.dtype)


def _resident(a):
    return pl.BlockSpec(a.shape, lambda i: (0,) * a.ndim, pipeline_mode=pl.Buffered(1))


def _rwkv(u, w_rw, mu, w0, w2, a0, a2, g2, k_k, k_a, r_k, ln_w, ln_b, tb=256):
    t, d = u.shape
    tb = min(tb, t)
    w = RW_WIDTH

    def full(a):
        return pl.BlockSpec(a.shape, lambda i: (0,) * a.ndim)

    params = (mu, w0, w2, a0, a2, g2, k_k, k_a, r_k, ln_w, ln_b)
    return pl.pallas_call(
        functools.partial(_rwkv_kernel, tb=tb),
        grid=(t // tb,),
        in_specs=[pl.BlockSpec((tb, d), lambda i: (i, 0)), _resident(w_rw)] + [full(a) for a in params],
        out_specs=pl.BlockSpec((tb, w), lambda i: (i, 0)),
        out_shape=jax.ShapeDtypeStruct((t, w), BF16),
        scratch_shapes=[pltpu.VMEM((SUBLANES, RW_PAD_COLS), F32),
                        pltpu.VMEM((RW_HEADS // 2, 2 * CHUNK, 2 * CHUNK), F32)]
        + [pltpu.VMEM((tb, w), BF16)] * 3 + [pltpu.VMEM((w, 2 * tb), BF16)] * 2
        + [pltpu.VMEM((tb, w), F32)] * 3,
        compiler_params=pltpu.CompilerParams(
            dimension_semantics=("arbitrary",), vmem_limit_bytes=VMEM_LIMIT),
        name="rwkv7",
    )(u, w_rw, *params)


def _gdn_kernel(u_ref, w_ref, cw_ref, alog_ref, dtb_ref, nw_ref, o_ref,
                carry_ref, st_ref, q_s, k_s, vb_s, kbe_s, qe_s, kh_s, o_s, *, tb):
    @pl.when(pl.program_id(0) == 0)
    def _():
        carry_ref[...] = jnp.zeros_like(carry_ref)
        st_ref[...] = jnp.zeros_like(st_ref)

    u = u_ref[...]
    cw = cw_ref[...]

    def conv_silu(c0, c1):
        xin = jnp.dot(u, w_ref[:, c0:c1], preferred_element_type=F32)
        carry = carry_ref[:, c0:c1]
        acc = xin * cw[GDN_CONV - 1:GDN_CONV, c0:c1]
        for j in range(1, GDN_CONV):
            acc = acc + _shift_rows(carry, xin, j) * cw[GDN_CONV - 1 - j:GDN_CONV - j, c0:c1]
        carry_ref[:, c0:c1] = xin[tb - SUBLANES:]
        return acc * _sigmoid(acc)

    def l2n(z):
        return z * lax.rsqrt(jnp.sum(z * z, axis=-1, keepdims=True) + 1e-6)

    hd = GDN_HEAD_DIM
    ba = jnp.dot(u, w_ref[:, GDN_OFF_BA:GDN_PAD_COLS], preferred_element_type=F32)
    kc = conv_silu(GDN_QK_WIDTH, 2 * GDN_QK_WIDTH)
    qc = conv_silu(0, GDN_QK_WIDTH)
    beta = _sigmoid(ba)
    gstep = -jnp.exp(alog_ref[...]) * _softplus(ba + dtb_ref[...])
    gc = _mm(_chunk_prefix_mask(tb), gstep, "nn", 1, 2)
    hh = _iota((LANES, GDN_V_WIDTH), 0)
    blk = lax.shift_right_logical(_iota((LANES, GDN_V_WIDTH), 1), hd.bit_length() - 1)
    e_beta = _ones_where(hh == blk)
    e_g = _ones_where(hh == blk + GDN_V_HEADS)
    bx = _dot(beta, e_beta)
    gcx = _mm(gc, e_g, "nn", 2, 1)
    egc = jnp.exp(gcx)
    L = CHUNK
    rep = GDN_V_HEADS // GDN_QK_HEADS
    hsl = [slice(h * hd, (h + 1) * hd) for h in range(GDN_V_HEADS)]

    for h in range(GDN_V_HEADS):
        sl = hsl[h]
        if h % rep == 0:
            kn = l2n(kc[:, hsl[h // rep]])
            k_s[:, hsl[h // rep]] = kn.astype(BF16)
        kbe_s[:, sl] = (kn * bx[:, sl] * egc[:, sl]).astype(BF16)
        for ci in range(tb // L):
            rs = slice(ci * L, (ci + 1) * L)
            to_end = jnp.exp(gcx[(ci + 1) * L - 1:(ci + 1) * L, sl] - gcx[rs, sl])
            kh_s[rs, sl] = (kn[rs] * to_end).astype(BF16)

    for h in range(GDN_V_HEADS):
        if h % rep == 0:
            qn = l2n(qc[:, hsl[h // rep]]) * (hd ** -0.5)
            q_s[:, hsl[h // rep]] = qn.astype(BF16)
        qe_s[:, hsl[h]] = (qn * egc[:, hsl[h]]).astype(BF16)

    def value_group():
        vb_s[...] = (conv_silu(2 * GDN_QK_WIDTH, GDN_CONV_CH) * bx).astype(BF16)
        yield

    P2 = 2 * L
    rr = _iota((P2, P2), 0)
    cc = _iota((P2, P2), 1)
    same_blk = ((rr ^ cc) & L) == 0
    lr = rr & (L - 1)
    lc = cc & (L - 1)
    strict_bd = same_blk & (lr > lc)
    incl_bd = same_blk & (lr >= lc)
    n_pairs = GDN_QK_HEADS

    n_chunks = tb // L

    def state_free(chunks, out):
        its = [(ci, p) for ci in chunks for p in range(n_pairs)]
        gm = []
        for ci, p in its:
            rows = slice(ci * L, (ci + 1) * L)
            kp = k_s[rows, hsl[p]]
            gm.append(_dot(jnp.concatenate([kp, q_s[rows, hsl[p]]], 0),
                           jnp.concatenate([kp, kp], 0), "nt"))
        yield
        a_bd = []
        a_qk = []
        for (ci, p), g in zip(its, gm):
            rows = slice(ci * L, (ci + 1) * L)
            h0, h1 = hsl[2 * p], hsl[2 * p + 1]
            gcol = jnp.concatenate([gcx[rows, h0], gcx[rows, h1]], 0)
            diff = gcol - gcol.T
            decay = jnp.where(incl_bd, jnp.exp(jnp.where(incl_bd, diff, 0.0)), 0.0)
            bcol = jnp.concatenate([bx[rows, h0], bx[rows, h1]], 0)
            a_bd.append(jnp.where(strict_bd, -(jnp.concatenate([g[:L], g[:L]], 0) * bcol * decay), 0.0))
            a_qk.append((jnp.concatenate([g[L:], g[L:]], 0) * decay).astype(BF16))
        inv = []
        yield from _unit_lower_inverses(a_bd, inv)
        for k, (ci, p) in enumerate(its):
            rows = slice(ci * L, (ci + 1) * L)
            h0, h1 = hsl[2 * p], hsl[2 * p + 1]
            rhs = jnp.concatenate([jnp.concatenate([vb_s[rows, h0], kbe_s[rows, h0]], 1),
                                   jnp.concatenate([vb_s[rows, h1], kbe_s[rows, h1]], 1)], 0)
            out[(ci, p)] = (_dot(inv[k], rhs), a_qk[k])
        yield

    state = [st_ref[h] for h in range(GDN_V_HEADS)]

    def state_tail(ci, pre):
        rows = slice(ci * L, (ci + 1) * L)

        def uw_of(h):
            return pre[(ci, h // 2)][0][(h % 2) * L:(h % 2 + 1) * L]

        x = [_dot(jnp.concatenate([uw_of(h)[:, hd:].astype(BF16), qe_s[rows, hsl[h]]], 0), state[h])
             for h in range(GDN_V_HEADS)]
        yield
        vnew = [(uw_of(h)[:, :hd] - x[h][:L]).astype(BF16) for h in range(GDN_V_HEADS)]
        for p in range(n_pairs):
            o_pair = _dot(pre[(ci, p)][1], jnp.concatenate([vnew[2 * p], vnew[2 * p + 1]], 0))
            o_s[rows, hsl[2 * p]] = x[2 * p][L:] + o_pair[:L]
            o_s[rows, hsl[2 * p + 1]] = x[2 * p + 1][L:] + o_pair[L:]
        yield
        for h in range(GDN_V_HEADS):
            state[h] = (state[h] * jnp.exp(gcx[(ci + 1) * L - 1:(ci + 1) * L, hsl[h]])
                        + _dot(kh_s[rows, hsl[h]], vnew[h], "tn"))
        yield

    gate = {}

    def gate_projections():
        for h in range(GDN_V_HEADS):
            gate[h] = jnp.dot(u, w_ref[:, GDN_OFF_Z + h * hd:GDN_OFF_Z + (h + 1) * hd],
                              preferred_element_type=F32)
            yield

    pre = {}
    _run_stages(state_free(list(range(n_chunks)), pre), value_group())
    _run_stages(_chain_stages(state_tail(ci, pre) for ci in range(n_chunks)), gate_projections())
    for h in range(GDN_V_HEADS):
        st_ref[h] = state[h]

    nw = nw_ref[...]
    for h in range(GDN_V_HEADS):
        sl = hsl[h]
        o = o_s[:, sl]
        on = o * lax.rsqrt(jnp.mean(o * o, axis=-1, keepdims=True) + RMS_EPS) * nw
        o_ref[:, sl] = (on * (gate[h] * _sigmoid(gate[h]))).astype(o_ref.dtype)


def _gdn(u, w_gdn, conv_w, alog_pad, dtb_pad, norm_w, tb=256):
    t, d = u.shape
    tb = min(tb, t)

    def full(a):
        return pl.BlockSpec(a.shape, lambda i: (0,) * a.ndim)

    params = (conv_w, alog_pad, dtb_pad, norm_w)
    qk = pltpu.VMEM((tb, GDN_QK_WIDTH), BF16)
    return pl.pallas_call(
        functools.partial(_gdn_kernel, tb=tb),
        grid=(t // tb,),
        in_specs=[pl.BlockSpec((tb, d), lambda i: (i, 0)), _resident(w_gdn)] + [full(a) for a in params],
        out_specs=pl.BlockSpec((tb, GDN_V_WIDTH), lambda i: (i, 0)),
        out_shape=jax.ShapeDtypeStruct((t, GDN_V_WIDTH), BF16),
        scratch_shapes=[pltpu.VMEM((SUBLANES, GDN_CONV_CH), F32),
                        pltpu.VMEM((GDN_V_HEADS, GDN_HEAD_DIM, GDN_HEAD_DIM), F32),
                        qk, qk]
        + [pltpu.VMEM((tb, GDN_V_WIDTH), BF16)] * 4 + [pltpu.VMEM((tb, GDN_V_WIDTH), F32)],
        compiler_params=pltpu.CompilerParams(
            dimension_semantics=("arbitrary",), vmem_limit_bytes=VMEM_LIMIT),
        name="gdn",
    )(u, w_gdn, *params)


def _merge_kernel(u_ref, ya_ref, yb_ref, wga_ref, wgb_ref, la_ref, lb_ref, o_ref):
    u = u_ref[...]
    ga = _sigmoid(jnp.dot(u, wga_ref[...], preferred_element_type=F32))
    gb = _sigmoid(jnp.dot(u, wgb_ref[...], preferred_element_type=F32))
    ma = jnp.dot(ya_ref[...], la_ref[...], preferred_element_type=F32)
    mb = jnp.dot(yb_ref[...], lb_ref[...], preferred_element_type=F32)
    o_ref[...] = (ga * ma + gb * mb).astype(o_ref.dtype)


def _merge(u, ya, yb, wga, wgb, la, lb, tm=1024, tn=512):
    t, d = u.shape
    tm = min(tm, t)
    bw = ya.shape[1]
    return pl.pallas_call(
        _merge_kernel,
        grid=(d // tn, t // tm),
        in_specs=[
            pl.BlockSpec((tm, d), lambda c, i: (i, 0)),
            pl.BlockSpec((tm, bw), lambda c, i: (i, 0)),
            pl.BlockSpec((tm, bw), lambda c, i: (i, 0)),
            pl.BlockSpec((d, tn), lambda c, i: (0, c)),
            pl.BlockSpec((d, tn), lambda c, i: (0, c)),
            pl.BlockSpec((bw, tn), lambda c, i: (0, c)),
            pl.BlockSpec((bw, tn), lambda c, i: (0, c)),
        ],
        out_specs=pl.BlockSpec((tm, tn), lambda c, i: (i, c)),
        out_shape=jax.ShapeDtypeStruct((t, d), BF16),
        compiler_params=pltpu.CompilerParams(
            dimension_semantics=("arbitrary", "arbitrary"), vmem_limit_bytes=VMEM_LIMIT),
        name="merge",
    )(u, ya, yb, wga, wgb, la, lb)


def _resid_kernel(x_ref, w_ref, r_ref, g_ref, h_ref, hn_ref):
    h = r_ref[...] + jnp.dot(x_ref[...], w_ref[...], preferred_element_type=F32)
    h_ref[...] = h
    hn_ref[...] = _rms(h, g_ref[...]).astype(BF16)


def _resid_proj(x, w, resid, g, name, tm=512):
    t, k = x.shape
    d = w.shape[1]
    tm = min(tm, t)
    row_blk = pl.BlockSpec((tm, d), lambda i: (i, 0))
    return pl.pallas_call(
        _resid_kernel,
        grid=(t // tm,),
        in_specs=[
            pl.BlockSpec((tm, k), lambda i: (i, 0)),
            pl.BlockSpec((k, d), lambda i: (0, 0)),
            row_blk,
            pl.BlockSpec((1, d), lambda i: (0, 0)),
        ],
        out_specs=[row_blk, row_blk],
        out_shape=[jax.ShapeDtypeStruct((t, d), F32), jax.ShapeDtypeStruct((t, d), BF16)],
        compiler_params=pltpu.CompilerParams(
            dimension_semantics=("arbitrary",), vmem_limit_bytes=VMEM_LIMIT),
        name=name,
    )(x, w, resid, g)


def _memkv_kernel(m_ref, g_ref, wk_ref, wv_ref, k_ref, v_ref):
    mn = _rms(m_ref[...], g_ref[...]).astype(BF16)
    k_ref[...] = jnp.dot(mn, wk_ref[...], preferred_element_type=F32).astype(BF16)
    v_ref[...] = jnp.dot(mn, wv_ref[...], preferred_element_type=F32).astype(BF16)


def _memkv(mem, g, wk, wv, tn=512):
    m, d = mem.shape
    return pl.pallas_call(
        _memkv_kernel,
        grid=(d // tn,),
        in_specs=[
            pl.BlockSpec((m, d), lambda c: (0, 0)),
            pl.BlockSpec((1, d), lambda c: (0, 0)),
            pl.BlockSpec((d, tn), lambda c: (0, c)),
            pl.BlockSpec((d, tn), lambda c: (0, c)),
        ],
        out_specs=[pl.BlockSpec((m, tn), lambda c: (0, c))] * 2,
        out_shape=[jax.ShapeDtypeStruct((m, d), BF16)] * 2,
        compiler_params=pltpu.CompilerParams(
            dimension_semantics=("arbitrary",), vmem_limit_bytes=VMEM_LIMIT),
        name="memkv",
    )(mem, g, wk, wv)


def _xattn_kernel(hq_ref, wq_ref, k_ref, v_ref, o_ref):
    q = jnp.dot(hq_ref[...], wq_ref[...], preferred_element_type=F32)
    hd = MEM_HEAD_DIM
    for h in range(MEM_HEADS):
        sl = slice(h * hd, (h + 1) * hd)
        s = _mm(q[:, sl].astype(BF16), k_ref[:, sl], "nt") * (hd ** -0.5)
        e = jnp.exp(s - jnp.max(s, axis=-1, keepdims=True))
        p = e / jnp.sum(e, axis=-1, keepdims=True)
        o_ref[:, sl] = jnp.dot(p.astype(BF16), v_ref[:, sl], preferred_element_type=F32).astype(BF16)


def _xattn(hq, wq, k, v, tm=512):
    t, d = hq.shape
    tm = min(tm, t)
    m = k.shape[0]
    return pl.pallas_call(
        _xattn_kernel,
        grid=(t // tm,),
        in_specs=[
            pl.BlockSpec((tm, d), lambda i: (i, 0)),
            pl.BlockSpec((d, d), lambda i: (0, 0)),
            pl.BlockSpec((m, d), lambda i: (0, 0)),
            pl.BlockSpec((m, d), lambda i: (0, 0)),
        ],
        out_specs=pl.BlockSpec((tm, d), lambda i: (i, 0)),
        out_shape=jax.ShapeDtypeStruct((t, d), BF16),
        compiler_params=pltpu.CompilerParams(
            dimension_semantics=("arbitrary",), vmem_limit_bytes=VMEM_LIMIT),
        name="xattn",
    )(hq, wq, k, v)


def _pad_cols(a, n):
    return jnp.pad(a, ((0, 0), (0, n - a.shape[1])))


def _pad_rows(a, n):
    return jnp.pad(a, ((0, n - a.shape[0]), (0, 0)))


def _rw_layout(a):
    o1 = 3 * RW_WIDTH
    o2 = o1 + RW_DECAY_LORA
    o3 = o2 + RW_AAA_LORA
    return jnp.concatenate([
        a[:, :o1],
        _pad_cols(a[:, o1:o2], LANES),
        _pad_cols(a[:, o2:o3], LANES),
        _pad_cols(a[:, o3:], 2 * LANES),
    ], axis=1)


def kernel(x, mem, ffn1_norm, ffn1_w_gate, ffn1_w_up, ffn1_w_down, mix_norm, w_in, rw_shift_mu, rw_w0, rw_w2, rw_a0, rw_a2, rw_g2, rw_k_k, rw_k_a, rw_r_k, rw_ln_w, rw_ln_b, gdn_conv_w, gdn_a_log, gdn_dt_bias, gdn_norm_w, w_lift, w_out, mem_q_norm, mem_kv_norm, w_mq, w_mk, w_mv, w_mo, ffn2_norm, ffn2_w_gate, ffn2_w_up, ffn2_w_down, final_norm):
    b_, t, d = x.shape
    assert b_ == 1 and d == D_MODEL and ffn1_norm.shape[0] == 1
    l = 0
    bf = lambda a: a.astype(BF16)
    row = lambda a: a.reshape(1, -1)

    h0 = x.reshape(t, d)
    h1, u = _ffn_emit(h0, row(ffn1_norm[l]), bf(ffn1_w_gate[l]), bf(ffn1_w_up[l]), bf(ffn1_w_down[l]),
                      row(mix_norm[l]))

    w_in_l = bf(w_in[l])
    w_rw = _rw_layout(w_in_l[:, :RW_COLS])
    w_gdn = _pad_cols(w_in_l[:, RW_COLS:RW_COLS + GDN_COLS], GDN_PAD_COLS)
    w_gate = w_in_l[:, RW_COLS + GDN_COLS:]
    wga = w_gate[:, :D_MODEL]
    wgb = w_gate[:, D_MODEL:]

    y_rw = _rwkv(
        u, w_rw, _rw_layout(row(rw_shift_mu[l])), row(rw_w0[l]), _pad_rows(rw_w2[l], LANES),
        row(rw_a0[l]), _pad_rows(rw_a2[l], LANES), _pad_rows(rw_g2[l], 2 * LANES),
        row(rw_k_k[l]), row(rw_k_a[l]), row(rw_r_k[l]), row(rw_ln_w[l]), row(rw_ln_b[l]))

    nh = GDN_V_HEADS
    alog_pad = jnp.pad(row(gdn_a_log[l]), ((0, 0), (nh, LANES - 2 * nh)))
    dtb_pad = jnp.pad(row(gdn_dt_bias[l]), ((0, 0), (nh, LANES - 2 * nh)))
    y_gdn = _gdn(u, w_gdn, gdn_conv_w[l], alog_pad, dtb_pad, row(gdn_norm_w[l]))

    merged = _merge(u, y_rw, y_gdn, wga, wgb, bf(w_lift[l, 0]), bf(w_lift[l, 1]))
    h2, hq = _resid_proj(merged, bf(w_out[l]), h1, row(mem_q_norm[l]), "out_proj")

    mk, mv = _memkv(mem.reshape(mem.shape[1], d), row(mem_kv_norm[l]), bf(w_mk[l]), bf(w_mv[l]))
    att = _xattn(hq, bf(w_mq[l]), mk, mv)
    h3, h3n = _resid_proj(att, bf(w_mo[l]), h2, row(ffn2_norm[l]), "attn_out")

    y = _ffn_final(h3, h3n, bf(ffn2_w_gate[l]), bf(ffn2_w_up[l]), bf(ffn2_w_down[l]), row(final_norm))
    return y.reshape(b_, t, d)
```

```python
import functools

import jax
import jax.numpy as jnp
from jax import lax
from jax.experimental import pallas as pl
from jax.experimental.pallas import tpu as pltpu

F32 = jnp.float32
BF16 = jnp.bfloat16

D_MODEL = 2048
D_FF = 5632
RMS_EPS = 1e-6

RW_HEADS = 16
RW_HEAD_DIM = 64
RW_WIDTH = 1024
RW_DECAY_LORA = 64
RW_AAA_LORA = 64
RW_GATE_LORA = 160
RW_GN_EPS = 64e-5
RW_COLS = 3 * RW_WIDTH + RW_DECAY_LORA + RW_AAA_LORA + RW_GATE_LORA
LANES = 128
RW_OFF_WL = 3 * RW_WIDTH
RW_OFF_AL = RW_OFF_WL + LANES
RW_OFF_GL = RW_OFF_AL + LANES
RW_PAD_COLS = RW_OFF_GL + 2 * LANES

GDN_QK_HEADS = 4
GDN_V_HEADS = 8
GDN_HEAD_DIM = 128
GDN_QK_WIDTH = 512
GDN_V_WIDTH = 1024
GDN_CONV = 4
GDN_CONV_CH = 2 * GDN_QK_WIDTH + GDN_V_WIDTH
GDN_COLS = GDN_CONV_CH + GDN_V_WIDTH + 2 * GDN_V_HEADS
GDN_OFF_Z = GDN_CONV_CH
GDN_OFF_BA = GDN_CONV_CH + GDN_V_WIDTH
GDN_PAD_COLS = GDN_OFF_BA + LANES

MEM_HEADS = 4
MEM_HEAD_DIM = D_MODEL // MEM_HEADS

CHUNK = 64
SUBLANES = 8
VMEM_LIMIT = 56 * 1024 * 1024

_DIMS = {
    "nn": (((1,), (0,)), ((), ())),
    "nt": (((1,), (1,)), ((), ())),
    "tn": (((0,), (0,)), ((), ())),
}


def _split_bf16(x, n):
    if x.dtype == BF16:
        return [x]
    parts = []
    r = x
    for i in range(n):
        p = r.astype(BF16)
        parts.append(p)
        if i + 1 < n:
            r = r - p.astype(F32)
    return parts


def _mm(a, b, dims="nn", pa=1, pb=1):
    ap = _split_bf16(a, pa)
    bp = _split_bf16(b, pb)
    n = max(len(ap), len(bp))
    acc = None
    for i, ai in enumerate(ap):
        for j, bj in enumerate(bp):
            if i + j < n:
                t = lax.dot_general(ai, bj, _DIMS[dims], preferred_element_type=F32)
                acc = t if acc is None else acc + t
    return acc


def _rms(x, g, eps=RMS_EPS):
    ms = jnp.mean(x * x, axis=-1, keepdims=True)
    return x * lax.rsqrt(ms + eps) * g


def _sigmoid(x):
    return 1.0 / (1.0 + jnp.exp(-x))


def _softplus(x):
    return jnp.maximum(x, 0.0) + jnp.log(1.0 + jnp.exp(-jnp.abs(x)))


def _iota(shape, axis):
    return lax.broadcasted_iota(jnp.int32, shape, axis)


def _ones_where(cond):
    return jnp.where(cond, 1.0, 0.0).astype(BF16)


def _dot(a, b, dims="nn"):
    return lax.dot_general(a.astype(BF16), b.astype(BF16), _DIMS[dims], preferred_element_type=F32)


def _group_sum(x, group):
    shift = group.bit_length() - 1
    wide = 2 * LANES
    gi = lax.shift_right_logical(_iota((wide, wide), 0), shift)
    gj = lax.shift_right_logical(_iota((wide, wide), 1), shift)
    bd = _ones_where(gi == gj)
    xb = x.astype(BF16)
    outs = [jnp.dot(xb[:, c * wide:(c + 1) * wide], bd, preferred_element_type=F32)
            for c in range(x.shape[-1] // wide)]
    return outs[0] if len(outs) == 1 else jnp.concatenate(outs, axis=-1)


def _chunk_prefix_mask(tb):
    shift = CHUNK.bit_length() - 1
    i = _iota((tb, tb), 0)
    j = _iota((tb, tb), 1)
    same = lax.shift_right_logical(i, shift) == lax.shift_right_logical(j, shift)
    return _ones_where(same & (i >= j))


def _shift_rows(carry, x, j):
    xx = jnp.concatenate([carry, x], axis=0)
    return pltpu.roll(xx, j, 0)[SUBLANES:]


def _unit_lower_inverses(mats, out):
    n = mats[0].shape[0]
    eye = jnp.where(_iota((n, n), 0) == _iota((n, n), 1), 1.0, 0.0)
    invs = [eye + a for a in mats]
    pws = [a.astype(BF16) for a in mats]
    for _ in range(CHUNK.bit_length() - 2):
        pws = [_dot(pw, pw).astype(BF16) for pw in pws]
        yield
        invs = [inv + _dot(inv, pw) for inv, pw in zip(invs, pws)]
        yield
    out.extend(invs)


def _run_stages(*gens):
    live = list(gens)
    while live:
        for g in list(live):
            try:
                next(g)
            except StopIteration:
                live.remove(g)


def _chain_stages(gens):
    for g in gens:
        yield from g


def _swiglu_down(xn, wg_ref, wu_ref, wd_ref):
    gate = jnp.dot(xn, wg_ref[...], preferred_element_type=F32)
    up = jnp.dot(xn, wu_ref[...], preferred_element_type=F32)
    mid = (gate * _sigmoid(gate) * up).astype(BF16)
    return jnp.dot(mid, wd_ref[...], preferred_element_type=F32)


def _ffn_emit_kernel(x_ref, g_ref, wg_ref, wu_ref, wd_ref, g2_ref, h_ref, u_ref, xn_ref, acc_ref, *, nj):
    j = pl.program_id(1)

    @pl.when(j == 0)
    def _():
        xn_ref[...] = _rms(x_ref[...], g_ref[...]).astype(BF16)
        acc_ref[...] = jnp.zeros_like(acc_ref)

    acc_ref[...] += _swiglu_down(xn_ref[...], wg_ref, wu_ref, wd_ref)

    @pl.when(j == nj - 1)
    def _():
        h = x_ref[...] + 0.5 * acc_ref[...]
        h_ref[...] = h
        u_ref[...] = _rms(h, g2_ref[...]).astype(BF16)


def _ffn_emit(x, g, wg, wu, wd, g2, tm=512, tf=512):
    t, d = x.shape
    nj = wg.shape[1] // tf
    row_blk = pl.BlockSpec((tm, d), lambda i, j: (i, 0))
    vec = pl.BlockSpec((1, d), lambda i, j: (0, 0))
    return pl.pallas_call(
        functools.partial(_ffn_emit_kernel, nj=nj),
        grid=(t // tm, nj),
        in_specs=[row_blk, vec,
                  pl.BlockSpec((d, tf), lambda i, j: (0, j)),
                  pl.BlockSpec((d, tf), lambda i, j: (0, j)),
                  pl.BlockSpec((tf, d), lambda i, j: (j, 0)),
                  vec],
        out_specs=[row_blk, row_blk],
        out_shape=[jax.ShapeDtypeStruct((t, d), F32), jax.ShapeDtypeStruct((t, d), BF16)],
        scratch_shapes=[pltpu.VMEM((tm, d), BF16), pltpu.VMEM((tm, d), F32)],
        compiler_params=pltpu.CompilerParams(
            dimension_semantics=("arbitrary", "arbitrary"), vmem_limit_bytes=VMEM_LIMIT),
        name="ffn_emit",
    )(x, g, wg, wu, wd, g2)


def _ffn_final_kernel(x_ref, xn_ref, wg_ref, wu_ref, wd_ref, g2_ref, o_ref, *, nj):
    j = pl.program_id(1)

    @pl.when(j == 0)
    def _():
        o_ref[...] = jnp.zeros_like(o_ref)

    o_ref[...] += _swiglu_down(xn_ref[...], wg_ref, wu_ref, wd_ref)

    @pl.when(j == nj - 1)
    def _():
        o_ref[...] = _rms(x_ref[...] + 0.5 * o_ref[...], g2_ref[...])


def _ffn_final(x, xn, wg, wu, wd, g2, tm=1024, tf=256):
    t, d = x.shape
    tm = min(tm, t)
    nj = wg.shape[1] // tf
    row_blk = pl.BlockSpec((tm, d), lambda i, j: (i, 0))
    return pl.pallas_call(
        functools.partial(_ffn_final_kernel, nj=nj),
        grid=(t // tm, nj),
        in_specs=[row_blk, row_blk,
                  pl.BlockSpec((d, tf), lambda i, j: (0, j)),
                  pl.BlockSpec((d, tf), lambda i, j: (0, j)),
                  pl.BlockSpec((tf, d), lambda i, j: (j, 0)),
                  pl.BlockSpec((1, d), lambda i, j: (0, 0))],
        out_specs=row_blk,
        out_shape=jax.ShapeDtypeStruct((t, d), F32),
        compiler_params=pltpu.CompilerParams(
            dimension_semantics=("arbitrary", "arbitrary"), vmem_limit_bytes=VMEM_LIMIT),
        name="ffn_final",
    )(x, xn, wg, wu, wd, g2)


def _rwkv_kernel(u_ref, w_ref, mu_ref, w0_ref, w2_ref, a0_ref, a2_ref, g2_ref, kk_ref, ka_ref, rk_ref,
                 lnw_ref, lnb_ref, o_ref,
                 carry_ref, st_ref, at_s, rt_s, v_s, bkT_s, kbT_s, y_s, bonus_s, g_s,
                 *, tb):
    @pl.when(pl.program_id(0) == 0)
    def _():
        carry_ref[...] = jnp.zeros_like(carry_ref)
        st_ref[...] = jnp.zeros_like(st_ref)

    u = u_ref[...]

    def mixed(c0, c1):
        cols = jnp.dot(u, w_ref[:, c0:c1], preferred_element_type=F32)
        prev = _shift_rows(carry_ref[:, c0:c1], cols, 1)
        carry_ref[:, c0:c1] = cols[tb - SUBLANES:]
        return cols + (prev - cols) * mu_ref[:, c0:c1]

    xl = mixed(RW_OFF_WL, RW_PAD_COLS)
    k = mixed(RW_WIDTH, 2 * RW_WIDTH)
    wl = xl[:, :LANES]
    al = xl[:, LANES:2 * LANES]
    gl = xl[:, 2 * LANES:]
    zw = w0_ref[...] + _dot(jnp.tanh(wl), w2_ref[...])
    w_log = -_softplus(-zw) - 0.5
    lw = -jnp.exp(w_log)
    a = _sigmoid(a0_ref[...] + _dot(al, a2_ref[...]))
    c = _mm(_chunk_prefix_mask(tb), lw, "nn", 1, 2)
    einv = jnp.exp(-c)

    r = mixed(0, RW_WIDTH)
    kkv = k * kk_ref[...]
    kkn = kkv * lax.rsqrt(_group_sum(kkv * kkv, RW_HEAD_DIM) + 1e-6)
    k2 = k * (1.0 + (a - 1.0) * ka_ref[...])
    at_s[...] = (-kkn * jnp.exp(c - lw)).astype(BF16)
    L = CHUNK
    P2 = 2 * L
    n_chunks = tb // L
    bt = kkn * a * einv
    kt = k2 * einv
    for ci in range(n_chunks):
        rs = slice(ci * L, (ci + 1) * L)
        bkT_s[:, ci * P2:(ci + 1) * P2] = jnp.concatenate([bt[rs], kt[rs]], 0).T.astype(BF16)
        kbT_s[:, ci * P2:(ci + 1) * P2] = jnp.concatenate([kt[rs], bt[rs]], 0).T.astype(BF16)
    c_end = jnp.concatenate([c[(ci + 1) * L - 1:(ci + 1) * L] for ci in range(n_chunks)]
                            + [jnp.zeros((P2 - n_chunks, RW_WIDTH), F32)], 0)
    w_end_t = jnp.exp(c_end.T)

    v = mixed(2 * RW_WIDTH, 3 * RW_WIDTH)
    rt_s[...] = (r * jnp.exp(c)).astype(BF16)
    v_s[...] = v.astype(BF16)

    def output_terms():
        g_s[...] = _dot(_sigmoid(gl), g2_ref[...])
        yield
        bonus_s[...] = _group_sum(r * k2 * rk_ref[...], RW_HEAD_DIM) * v
        yield

    n_pairs = RW_HEADS // 2
    lane_l = _iota((L, P2), 1)
    row_l = _iota((L, P2), 0)
    h0_l = lane_l < L
    strict_l = row_l > (lane_l & (L - 1))
    incl_l = row_l >= (lane_l & (L - 1))
    rr = _iota((P2, P2), 0)
    cc = _iota((P2, P2), 1)
    same_blk = ((rr ^ cc) & L) == 0
    strict_bd = same_blk & ((rr & (L - 1)) > (cc & (L - 1)))
    top = rr < L

    def only0(z):
        return jnp.where(h0_l, z, jnp.zeros_like(z))

    def only1(z):
        return jnp.where(h0_l, jnp.zeros_like(z), z)

    n_chunks = tb // L
    sls = [slice(p * P2, (p + 1) * P2) for p in range(n_pairs)]

    def ld(ref, it):
        return ref[it[0] * L:(it[0] + 1) * L, sls[it[1]]]

    def ld_t(ref, it):
        return ref[sls[it[1]], it[0] * P2:(it[0] + 1) * P2]

    def state_free(chunks, out):
        its = [(ci, p) for ci in chunks for p in range(n_pairs)]
        g0 = [_dot(jnp.concatenate([only0(ld(at_s, it)), only0(ld(rt_s, it))], 0), ld_t(bkT_s, it))
              for it in its]
        yield
        g1 = [_dot(jnp.concatenate([only1(ld(rt_s, it)), only1(ld(at_s, it))], 0), ld_t(kbT_s, it))
              for it in its]
        yield
        inv = []
        yield from _unit_lower_inverses(
            [jnp.where(strict_bd, jnp.where(top, a0_, a1_), 0.0) for a0_, a1_ in zip(g0, g1)], inv)
        akv = [_dot(jnp.where(strict_l, jnp.where(h0_l, a1_[L:], a0_[:L]), 0.0),
                    jnp.concatenate([only1(ld(v_s, it)), only0(ld(v_s, it))], 0))
               for it, a0_, a1_ in zip(its, g0, g1)]
        yield
        for k, it in enumerate(its):
            ly = jnp.concatenate([jnp.where(incl_l, g0[k][L:], 0.0), jnp.where(incl_l, g1[k][:L], 0.0)], 1)
            out[it] = (inv[k].astype(BF16), akv[k], ly.astype(BF16))

    state = [st_ref[p] for p in range(n_pairs)]

    def state_tail(ci, pre):
        rows = slice(ci * L, (ci + 1) * L)
        its = [(ci, p) for p in range(n_pairs)]
        xs = [_dot(jnp.concatenate([ld(at_s, it), ld(rt_s, it)], 0), state[p])
              for p, it in enumerate(its)]
        yield
        rhs = [xs[p][:L] + pre[it][1] for p, it in enumerate(its)]
        pp = [_dot(pre[it][0], jnp.concatenate([only0(rhs[p]), only1(rhs[p])], 0)) for p, it in enumerate(its)]
        yield
        pm = [(z[:L] + z[L:]).astype(BF16) for z in pp]
        for p, it in enumerate(its):
            vv = ld(v_s, it)
            ry = jnp.concatenate([only0(pm[p]), only0(vv), only1(vv), only1(pm[p])], 0)
            y_s[rows, sls[p]] = xs[p][L:] + _dot(pre[it][2], ry)
        yield
        for p, it in enumerate(its):
            upd = _dot(ld_t(bkT_s, it), jnp.concatenate([pm[p], ld(v_s, it)], 0))
            state[p] = (state[p] + jnp.where(same_blk, upd, 0.0)) * w_end_t[sls[p], ci:ci + 1]
        yield

    pre = {}
    first = list(range(max(1, n_chunks - 1)))
    rest = list(range(len(first), n_chunks))
    _run_stages(state_free(first, pre))
    _run_stages(state_free(rest, pre) if rest else iter(()),
                _chain_stages(state_tail(ci, pre) for ci in first))
    _run_stages(_chain_stages(state_tail(ci, pre) for ci in rest), output_terms())
    for p in range(n_pairs):
        st_ref[p] = state[p]

    y = y_s[...]
    inv_n = 1.0 / RW_HEAD_DIM
    mu = _group_sum(y, RW_HEAD_DIM) * inv_n
    yc = y - mu
    var = _group_sum(yc * yc, RW_HEAD_DIM) * inv_n
    yn = yc * lax.rsqrt(var + RW_GN_EPS) * lnw_ref[...] + lnb_ref[...]
    o_ref[...] = ((yn + bonus_s[...]) * g_s[...]).astype(o_ref.dtype)


def _resident(a):
    return pl.BlockSpec(a.shape, lambda i: (0,) * a.ndim, pipeline_mode=pl.Buffered(1))


def _rwkv(u, w_rw, mu, w0, w2, a0, a2, g2, k_k, k_a, r_k, ln_w, ln_b, tb=256):
    t, d = u.shape
    tb = min(tb, t)
    w = RW_WIDTH

    def full(a):
        return pl.BlockSpec(a.shape, lambda i: (0,) * a.ndim)

    params = (mu, w0, w2, a0, a2, g2, k_k, k_a, r_k, ln_w, ln_b)
    return pl.pallas_call(
        functools.partial(_rwkv_kernel, tb=tb),
        grid=(t // tb,),
        in_specs=[pl.BlockSpec((tb, d), lambda i: (i, 0)), _resident(w_rw)] + [full(a) for a in params],
        out_specs=pl.BlockSpec((tb, w), lambda i: (i, 0)),
        out_shape=jax.ShapeDtypeStruct((t, w), BF16),
        scratch_shapes=[pltpu.VMEM((SUBLANES, RW_PAD_COLS), F32),
                        pltpu.VMEM((RW_HEADS // 2, 2 * CHUNK, 2 * CHUNK), F32)]
        + [pltpu.VMEM((tb, w), BF16)] * 3 + [pltpu.VMEM((w, 2 * tb), BF16)] * 2
        + [pltpu.VMEM((tb, w), F32)] * 3,
        compiler_params=pltpu.CompilerParams(
            dimension_semantics=("arbitrary",), vmem_limit_bytes=VMEM_LIMIT),
        name="rwkv7",
    )(u, w_rw, *params)


def _gdn_kernel(u_ref, w_ref, cw_ref, alog_ref, dtb_ref, nw_ref, o_ref,
                carry_ref, st_ref, q_s, k_s, vb_s, kbe_s, qe_s, kh_s, o_s, *, tb):
    @pl.when(pl.program_id(0) == 0)
    def _():
        carry_ref[...] = jnp.zeros_like(carry_ref)
        st_ref[...] = jnp.zeros_like(st_ref)

    u = u_ref[...]
    cw = cw_ref[...]

    def conv_silu(c0, c1):
        xin = jnp.dot(u, w_ref[:, c0:c1], preferred_element_type=F32)
        carry = carry_ref[:, c0:c1]
        acc = xin * cw[GDN_CONV - 1:GDN_CONV, c0:c1]
        for j in range(1, GDN_CONV):
            acc = acc + _shift_rows(carry, xin, j) * cw[GDN_CONV - 1 - j:GDN_CONV - j, c0:c1]
        carry_ref[:, c0:c1] = xin[tb - SUBLANES:]
        return acc * _sigmoid(acc)

    def l2n(z):
        return z * lax.rsqrt(jnp.sum(z * z, axis=-1, keepdims=True) + 1e-6)

    hd = GDN_HEAD_DIM
    ba = jnp.dot(u, w_ref[:, GDN_OFF_BA:GDN_PAD_COLS], preferred_element_type=F32)
    kc = conv_silu(GDN_QK_WIDTH, 2 * GDN_QK_WIDTH)
    qc = conv_silu(0, GDN_QK_WIDTH)
    beta = _sigmoid(ba)
    gstep = -jnp.exp(alog_ref[...]) * _softplus(ba + dtb_ref[...])
    gc = _mm(_chunk_prefix_mask(tb), gstep, "nn", 1, 2)
    hh = _iota((LANES, GDN_V_WIDTH), 0)
    blk = lax.shift_right_logical(_iota((LANES, GDN_V_WIDTH), 1), hd.bit_length() - 1)
    e_beta = _ones_where(hh == blk)
    e_g = _ones_where(hh == blk + GDN_V_HEADS)
    bx = _dot(beta, e_beta)
    gcx = _mm(gc, e_g, "nn", 2, 1)
    egc = jnp.exp(gcx)
    L = CHUNK
    rep = GDN_V_HEADS // GDN_QK_HEADS
    hsl = [slice(h * hd, (h + 1) * hd) for h in range(GDN_V_HEADS)]

    for h in range(GDN_V_HEADS):
        sl = hsl[h]
        if h % rep == 0:
            kn = l2n(kc[:, hsl[h // rep]])
            k_s[:, hsl[h // rep]] = kn.astype(BF16)
        kbe_s[:, sl] = (kn * bx[:, sl] * egc[:, sl]).astype(BF16)
        for ci in range(tb // L):
            rs = slice(ci * L, (ci + 1) * L)
            to_end = jnp.exp(gcx[(ci + 1) * L - 1:(ci + 1) * L, sl] - gcx[rs, sl])
            kh_s[rs, sl] = (kn[rs] * to_end).astype(BF16)

    for h in range(GDN_V_HEADS):
        if h % rep == 0:
            qn = l2n(qc[:, hsl[h // rep]]) * (hd ** -0.5)
            q_s[:, hsl[h // rep]] = qn.astype(BF16)
        qe_s[:, hsl[h]] = (qn * egc[:, hsl[h]]).astype(BF16)

    def value_group():
        vb_s[...] = (conv_silu(2 * GDN_QK_WIDTH, GDN_CONV_CH) * bx).astype(BF16)
        yield

    P2 = 2 * L
    rr = _iota((P2, P2), 0)
    cc = _iota((P2, P2), 1)
    same_blk = ((rr ^ cc) & L) == 0
    lr = rr & (L - 1)
    lc = cc & (L - 1)
    strict_bd = same_blk & (lr > lc)
    incl_bd = same_blk & (lr >= lc)
    n_pairs = GDN_QK_HEADS

    n_chunks = tb // L

    def state_free(chunks, out):
        its = [(ci, p) for ci in chunks for p in range(n_pairs)]
        gm = []
        for ci, p in its:
            rows = slice(ci * L, (ci + 1) * L)
            kp = k_s[rows, hsl[p]]
            gm.append(_dot(jnp.concatenate([kp, q_s[rows, hsl[p]]], 0),
                           jnp.concatenate([kp, kp], 0), "nt"))
        yield
        a_bd = []
        a_qk = []
        for (ci, p), g in zip(its, gm):
            rows = slice(ci * L, (ci + 1) * L)
            h0, h1 = hsl[2 * p], hsl[2 * p + 1]
            gcol = jnp.concatenate([gcx[rows, h0], gcx[rows, h1]], 0)
            diff = gcol - gcol.T
            decay = jnp.where(incl_bd, jnp.exp(jnp.where(incl_bd, diff, 0.0)), 0.0)
            bcol = jnp.concatenate([bx[rows, h0], bx[rows, h1]], 0)
            a_bd.append(jnp.where(strict_bd, -(jnp.concatenate([g[:L], g[:L]], 0) * bcol * decay), 0.0))
            a_qk.append((jnp.concatenate([g[L:], g[L:]], 0) * decay).astype(BF16))
        inv = []
        yield from _unit_lower_inverses(a_bd, inv)
        for k, (ci, p) in enumerate(its):
            rows = slice(ci * L, (ci + 1) * L)
            h0, h1 = hsl[2 * p], hsl[2 * p + 1]
            rhs = jnp.concatenate([jnp.concatenate([vb_s[rows, h0], kbe_s[rows, h0]], 1),
                                   jnp.concatenate([vb_s[rows, h1], kbe_s[rows, h1]], 1)], 0)
            out[(ci, p)] = (_dot(inv[k], rhs), a_qk[k])
        yield

    state = [st_ref[h] for h in range(GDN_V_HEADS)]

    def state_tail(ci, pre):
        rows = slice(ci * L, (ci + 1) * L)

        def uw_of(h):
            return pre[(ci, h // 2)][0][(h % 2) * L:(h % 2 + 1) * L]

        x = [_dot(jnp.concatenate([uw_of(h)[:, hd:].astype(BF16), qe_s[rows, hsl[h]]], 0), state[h])
             for h in range(GDN_V_HEADS)]
        yield
        vnew = [(uw_of(h)[:, :hd] - x[h][:L]).astype(BF16) for h in range(GDN_V_HEADS)]
        for p in range(n_pairs):
            o_pair = _dot(pre[(ci, p)][1], jnp.concatenate([vnew[2 * p], vnew[2 * p + 1]], 0))
            o_s[rows, hsl[2 * p]] = x[2 * p][L:] + o_pair[:L]
            o_s[rows, hsl[2 * p + 1]] = x[2 * p + 1][L:] + o_pair[L:]
        yield
        for h in range(GDN_V_HEADS):
            state[h] = (state[h] * jnp.exp(gcx[(ci + 1) * L - 1:(ci + 1) * L, hsl[h]])
                        + _dot(kh_s[rows, hsl[h]], vnew[h], "tn"))
        yield

    gate = {}

    def gate_projections():
        for h in range(GDN_V_HEADS):
            gate[h] = jnp.dot(u, w_ref[:, GDN_OFF_Z + h * hd:GDN_OFF_Z + (h + 1) * hd],
                              preferred_element_type=F32)
            yield

    pre = {}
    _run_stages(state_free(list(range(n_chunks)), pre), value_group())
    _run_stages(_chain_stages(state_tail(ci, pre) for ci in range(n_chunks)), gate_projections())
    for h in range(GDN_V_HEADS):
        st_ref[h] = state[h]

    nw = nw_ref[...]
    for h in range(GDN_V_HEADS):
        sl = hsl[h]
        o = o_s[:, sl]
        on = o * lax.rsqrt(jnp.mean(o * o, axis=-1, keepdims=True) + RMS_EPS) * nw
        o_ref[:, sl] = (on * (gate[h] * _sigmoid(gate[h]))).astype(o_ref.dtype)


def _gdn(u, w_gdn, conv_w, alog_pad, dtb_pad, norm_w, tb=256):
    t, d = u.shape
    tb = min(tb, t)

    def full(a):
        return pl.BlockSpec(a.shape, lambda i: (0,) * a.ndim)

    params = (conv_w, alog_pad, dtb_pad, norm_w)
    qk = pltpu.VMEM((tb, GDN_QK_WIDTH), BF16)
    return pl.pallas_call(
        functools.partial(_gdn_kernel, tb=tb),
        grid=(t // tb,),
        in_specs=[pl.BlockSpec((tb, d), lambda i: (i, 0)), _resident(w_gdn)] + [full(a) for a in params],
        out_specs=pl.BlockSpec((tb, GDN_V_WIDTH), lambda i: (i, 0)),
        out_shape=jax.ShapeDtypeStruct((t, GDN_V_WIDTH), BF16),
        scratch_shapes=[pltpu.VMEM((SUBLANES, GDN_CONV_CH), F32),
                        pltpu.VMEM((GDN_V_HEADS, GDN_HEAD_DIM, GDN_HEAD_DIM), F32),
                        qk, qk]
        + [pltpu.VMEM((tb, GDN_V_WIDTH), BF16)] * 4 + [pltpu.VMEM((tb, GDN_V_WIDTH), F32)],
        compiler_params=pltpu.CompilerParams(
            dimension_semantics=("arbitrary",), vmem_limit_bytes=VMEM_LIMIT),
        name="gdn",
    )(u, w_gdn, *params)


def _merge_kernel(u_ref, ya_ref, yb_ref, wga_ref, wgb_ref, la_ref, lb_ref, o_ref):
    u = u_ref[...]
    ga = _sigmoid(jnp.dot(u, wga_ref[...], preferred_element_type=F32))
    gb = _sigmoid(jnp.dot(u, wgb_ref[...], preferred_element_type=F32))
    ma = jnp.dot(ya_ref[...], la_ref[...], preferred_element_type=F32)
    mb = jnp.dot(yb_ref[...], lb_ref[...], preferred_element_type=F32)
    o_ref[...] = (ga * ma + gb * mb).astype(o_ref.dtype)


def _merge(u, ya, yb, wga, wgb, la, lb, tm=1024, tn=512):
    t, d = u.shape
    tm = min(tm, t)
    bw = ya.shape[1]
    return pl.pallas_call(
        _merge_kernel,
        grid=(d // tn, t // tm),
        in_specs=[
            pl.BlockSpec((tm, d), lambda c, i: (i, 0)),
            pl.BlockSpec((tm, bw), lambda c, i: (i, 0)),
            pl.BlockSpec((tm, bw), lambda c, i: (i, 0)),
            pl.BlockSpec((d, tn), lambda c, i: (0, c)),
            pl.BlockSpec((d, tn), lambda c, i: (0, c)),
            pl.BlockSpec((bw, tn), lambda c, i: (0, c)),
            pl.BlockSpec((bw, tn), lambda c, i: (0, c)),
        ],
        out_specs=pl.BlockSpec((tm, tn), lambda c, i: (i, c)),
        out_shape=jax.ShapeDtypeStruct((t, d), BF16),
        compiler_params=pltpu.CompilerParams(
            dimension_semantics=("arbitrary", "arbitrary"), vmem_limit_bytes=VMEM_LIMIT),
        name="merge",
    )(u, ya, yb, wga, wgb, la, lb)


def _resid_kernel(x_ref, w_ref, r_ref, g_ref, h_ref, hn_ref):
    h = r_ref[...] + jnp.dot(x_ref[...], w_ref[...], preferred_element_type=F32)
    h_ref[...] = h
    hn_ref[...] = _rms(h, g_ref[...]).astype(BF16)


def _resid_proj(x, w, resid, g, name, tm=512):
    t, k = x.shape
    d = w.shape[1]
    tm = min(tm, t)
    row_blk = pl.BlockSpec((tm, d), lambda i: (i, 0))
    return pl.pallas_call(
        _resid_kernel,
        grid=(t // tm,),
        in_specs=[
            pl.BlockSpec((tm, k), lambda i: (i, 0)),
            pl.BlockSpec((k, d), lambda i: (0, 0)),
            row_blk,
            pl.BlockSpec((1, d), lambda i: (0, 0)),
        ],
        out_specs=[row_blk, row_blk],
        out_shape=[jax.ShapeDtypeStruct((t, d), F32), jax.ShapeDtypeStruct((t, d), BF16)],
        compiler_params=pltpu.CompilerParams(
            dimension_semantics=("arbitrary",), vmem_limit_bytes=VMEM_LIMIT),
        name=name,
    )(x, w, resid, g)


def _memkv_kernel(m_ref, g_ref, wk_ref, wv_ref, k_ref, v_ref):
    mn = _rms(m_ref[...], g_ref[...]).astype(BF16)
    k_ref[...] = jnp.dot(mn, wk_ref[...], preferred_element_type=F32).astype(BF16)
    v_ref[...] = jnp.dot(mn, wv_ref[...], preferred_element_type=F32).astype(BF16)


def _memkv(mem, g, wk, wv, tn=512):
    m, d = mem.shape
    return pl.pallas_call(
        _memkv_kernel,
        grid=(d // tn,),
        in_specs=[
            pl.BlockSpec((m, d), lambda c: (0, 0)),
            pl.BlockSpec((1, d), lambda c: (0, 0)),
            pl.BlockSpec((d, tn), lambda c: (0, c)),
            pl.BlockSpec((d, tn), lambda c: (0, c)),
        ],
        out_specs=[pl.BlockSpec((m, tn), lambda c: (0, c))] * 2,
        out_shape=[jax.ShapeDtypeStruct((m, d), BF16)] * 2,
        compiler_params=pltpu.CompilerParams(
            dimension_semantics=("arbitrary",), vmem_limit_bytes=VMEM_LIMIT),
        name="memkv",
    )(mem, g, wk, wv)


def _xattn_kernel(hq_ref, wq_ref, k_ref, v_ref, o_ref):
    q = jnp.dot(hq_ref[...], wq_ref[...], preferred_element_type=F32)
    hd = MEM_HEAD_DIM
    for h in range(MEM_HEADS):
        sl = slice(h * hd, (h + 1) * hd)
        s = _mm(q[:, sl].astype(BF16), k_ref[:, sl], "nt") * (hd ** -0.5)
        e = jnp.exp(s - jnp.max(s, axis=-1, keepdims=True))
        p = e / jnp.sum(e, axis=-1, keepdims=True)
        o_ref[:, sl] = jnp.dot(p.astype(BF16), v_ref[:, sl], preferred_element_type=F32).astype(BF16)


def _xattn(hq, wq, k, v, tm=512):
    t, d = hq.shape
    tm = min(tm, t)
    m = k.shape[0]
    return pl.pallas_call(
        _xattn_kernel,
        grid=(t // tm,),
        in_specs=[
            pl.BlockSpec((tm, d), lambda i: (i, 0)),
            pl.BlockSpec((d, d), lambda i: (0, 0)),
            pl.BlockSpec((m, d), lambda i: (0, 0)),
            pl.BlockSpec((m, d), lambda i: (0, 0)),
        ],
        out_specs=pl.BlockSpec((tm, d), lambda i: (i, 0)),
        out_shape=jax.ShapeDtypeStruct((t, d), BF16),
        compiler_params=pltpu.CompilerParams(
            dimension_semantics=("arbitrary",), vmem_limit_bytes=VMEM_LIMIT),
        name="xattn",
    )(hq, wq, k, v)


def _pad_cols(a, n):
    return jnp.pad(a, ((0, 0), (0, n - a.shape[1])))


def _pad_rows(a, n):
    return jnp.pad(a, ((0, n - a.shape[0]), (0, 0)))


def _rw_layout(a):
    o1 = 3 * RW_WIDTH
    o2 = o1 + RW_DECAY_LORA
    o3 = o2 + RW_AAA_LORA
    return jnp.concatenate([
        a[:, :o1],
        _pad_cols(a[:, o1:o2], LANES),
        _pad_cols(a[:, o2:o3], LANES),
        _pad_cols(a[:, o3:], 2 * LANES),
    ], axis=1)


def kernel(x, mem, ffn1_norm, ffn1_w_gate, ffn1_w_up, ffn1_w_down, mix_norm, w_in, rw_shift_mu, rw_w0, rw_w2, rw_a0, rw_a2, rw_g2, rw_k_k, rw_k_a, rw_r_k, rw_ln_w, rw_ln_b, gdn_conv_w, gdn_a_log, gdn_dt_bias, gdn_norm_w, w_lift, w_out, mem_q_norm, mem_kv_norm, w_mq, w_mk, w_mv, w_mo, ffn2_norm, ffn2_w_gate, ffn2_w_up, ffn2_w_down, final_norm):
    b_, t, d = x.shape
    assert b_ == 1 and d == D_MODEL and ffn1_norm.shape[0] == 1
    l = 0
    bf = lambda a: a.astype(BF16)
    row = lambda a: a.reshape(1, -1)

    h0 = x.reshape(t, d)
    h1, u = _ffn_emit(h0, row(ffn1_norm[l]), bf(ffn1_w_gate[l]), bf(ffn1_w_up[l]), bf(ffn1_w_down[l]),
                      row(mix_norm[l]))

    w_in_l = bf(w_in[l])
    w_rw = _rw_layout(w_in_l[:, :RW_COLS])
    w_gdn = _pad_cols(w_in_l[:, RW_COLS:RW_COLS + GDN_COLS], GDN_PAD_COLS)
    w_gate = w_in_l[:, RW_COLS + GDN_COLS:]
    wga = w_gate[:, :D_MODEL]
    wgb = w_gate[:, D_MODEL:]

    y_rw = _rwkv(
        u, w_rw, _rw_layout(row(rw_shift_mu[l])), row(rw_w0[l]), _pad_rows(rw_w2[l], LANES),
        row(rw_a0[l]), _pad_rows(rw_a2[l], LANES), _pad_rows(rw_g2[l], 2 * LANES),
        row(rw_k_k[l]), row(rw_k_a[l]), row(rw_r_k[l]), row(rw_ln_w[l]), row(rw_ln_b[l]))

    nh = GDN_V_HEADS
    alog_pad = jnp.pad(row(gdn_a_log[l]), ((0, 0), (nh, LANES - 2 * nh)))
    dtb_pad = jnp.pad(row(gdn_dt_bias[l]), ((0, 0), (nh, LANES - 2 * nh)))
    y_gdn = _gdn(u, w_gdn, gdn_conv_w[l], alog_pad, dtb_pad, row(gdn_norm_w[l]))

    merged = _merge(u, y_rw, y_gdn, wga, wgb, bf(w_lift[l, 0]), bf(w_lift[l, 1]))
    h2, hq = _resid_proj(merged, bf(w_out[l]), h1, row(mem_q_norm[l]), "out_proj")

    mk, mv = _memkv(mem.reshape(mem.shape[1], d), row(mem_kv_norm[l]), bf(w_mk[l]), bf(w_mv[l]))
    att = _xattn(hq, bf(w_mq[l]), mk, mv)
    h3, h3n = _resid_proj(att, bf(w_mo[l]), h2, row(ffn2_norm[l]), "attn_out")

    y = _ffn_final(h3, h3n, bf(ffn2_w_gate[l]), bf(ffn2_w_up[l]), bf(ffn2_w_down[l]), row(final_norm))
    return y.reshape(b_, t, d)
```
